```python
import math
import jax, jax.numpy as jnp
from jax import lax
import numpy as np

D_MODEL = 4096
BATCH = 2
SEQ = 4096
DEPTH = 2

N_A_LAYERS = DEPTH // 2
N_B_LAYERS = DEPTH - N_A_LAYERS

A_GROUPS = ((128, 1), (512, 4), (2048, 16))
N_A_GROUPS = 3
A_HEADS_PER_GROUP = 16
A_HEAD_DIM = 128
A_QKV_WIDTH = 3 * N_A_GROUPS * A_HEADS_PER_GROUP * A_HEAD_DIM
A_OUT_WIDTH = A_HEADS_PER_GROUP * A_HEAD_DIM

N_BUCKETS = 32
REL_MAX_EXACT = N_BUCKETS // 2
REL_MAX_DISTANCE = 2048

B_HEADS = 32
Q_RANK = 1024
KV_RANK = 512
NOPE_DIM = 128
ROPE_DIM = 64
V_DIM = 128
QK_DIM = NOPE_DIM + ROPE_DIM
ROPE_THETA = 10000.0
MLA_BLOCK = 128

D_FF = 11008
CONV_WIDTH = 3

EPS = 1e-6

kernel_name = "yoco_dilated_mla_convffn_trunk"


def rmsnorm(x, gain):
    xf = x.astype(jnp.float32)
    y = xf * lax.rsqrt(jnp.mean(xf * xf, axis=-1, keepdims=True) + EPS)
    return (y * gain.astype(jnp.float32)).astype(x.dtype)


def t5_bucket(dist):
    small = dist < REL_MAX_EXACT
    d_f = jnp.maximum(dist, 1).astype(jnp.float32)
    large = REL_MAX_EXACT + (jnp.log(d_f / REL_MAX_EXACT) / math.log(REL_MAX_DISTANCE / REL_MAX_EXACT)
                             * (N_BUCKETS - REL_MAX_EXACT)).astype(jnp.int32)
    large = jnp.minimum(large, N_BUCKETS - 1)
    return jnp.where(small, dist, large)


def dilated_group(q, k, v, bias_tab, window, dilation):
    B_, S_, H, Dh = q.shape
    C = window // dilation
    n = S_ // dilation
    nc = -(-n // C)
    npad = nc * C

    def split(t):
        t = t.reshape(B_, n, dilation, H, Dh)
        t = jnp.pad(t, ((0, 0), (0, npad - n), (0, 0), (0, 0), (0, 0)))
        return t.reshape(B_, nc, C, dilation, H, Dh)

    def with_prev(t):
        prev = jnp.pad(t, ((0, 0), (1, 0), (0, 0), (0, 0), (0, 0), (0, 0)))[:, :-1]
        return jnp.concatenate([prev, t], axis=2)

    qc = split(q)
    kw = with_prev(split(k))
    vw = with_prev(split(v))

    logits = jnp.einsum('bcqrhd,bckrhd->bcrhqk', qc, kw).astype(jnp.float32) * (Dh ** -0.5)

    qi = jnp.arange(C, dtype=jnp.int32)[:, None]
    kk = jnp.arange(2 * C, dtype=jnp.int32)[None, :]
    steps = qi + C - kk
    band = (steps >= 0) & (steps <= C)
    bucket = t5_bucket(jnp.maximum(steps, 0) * dilation)
    bias = jnp.transpose(bias_tab[bucket].astype(jnp.float32), (2, 0, 1))
    blk = jnp.arange(nc, dtype=jnp.int32)[:, None]
    key_valid = (blk * C - C + kk) >= 0
    mask = band[None] & key_valid[:, None, :]

    logits = jnp.where(mask[None, :, None, None], logits + bias, -jnp.inf)
    m = jnp.max(logits, axis=-1, keepdims=True)
    p = jnp.exp(logits - m)
    l = jnp.sum(p, axis=-1, keepdims=True)
    lse = (m + jnp.log(l))[..., 0]
    o = jnp.einsum('bcrhqk,bckrhd->bcqrhd', (p / l).astype(v.dtype), vw)
    o = o.reshape(B_, npad, dilation, H, Dh)[:, :n].reshape(B_, S_, H, Dh)
    lse = jnp.transpose(lse, (0, 1, 4, 2, 3)).reshape(B_, npad, dilation, H)[:, :n].reshape(B_, S_, H)
    return o, lse


def dilated_mixer(h, norm_g, w_qkv, q_gain, k_gain, w_o, rel_bias):
    B_, S_, _ = h.shape
    hn = rmsnorm(h, norm_g)
    qkv = (hn @ w_qkv).reshape(B_, S_, 3, N_A_GROUPS, A_HEADS_PER_GROUP, A_HEAD_DIM)
    outs = []
    lses = []
    for g, (window, dilation) in enumerate(A_GROUPS):
        q = rmsnorm(qkv[:, :, 0, g], q_gain[g])
        k = rmsnorm(qkv[:, :, 1, g], k_gain[g])
        v = qkv[:, :, 2, g]
        o, lse = dilated_group(q, k, v, rel_bias[:, g], window, dilation)
        outs.append(o)
        lses.append(lse)
    mix = jax.nn.softmax(jnp.stack(lses, axis=0), axis=0)
    o = jnp.einsum('gbsh,gbshd->bshd', mix.astype(h.dtype), jnp.stack(outs, axis=0))
    return (o.reshape(B_, S_, A_OUT_WIDTH) @ w_o).astype(h.dtype)


def rope_tables(S_):
    pos = jnp.arange(S_, dtype=jnp.float32)
    inv = ROPE_THETA ** (-jnp.arange(0, ROPE_DIM, 2, dtype=jnp.float32) / ROPE_DIM)
    ang = pos[:, None] * inv[None, :]
    return jnp.cos(ang), jnp.sin(ang)


def apply_rope(t, cos, sin):
    tf = t.astype(jnp.float32)
    half = ROPE_DIM // 2
    t1, t2 = tf[..., :half], tf[..., half:]
    c = cos[None, :, None, :]
    s = sin[None, :, None, :]
    return jnp.concatenate([t1 * c - t2 * s, t2 * c + t1 * s], axis=-1).astype(t.dtype)


def qk_norm_rope(t, gain, cos, sin):
    t = rmsnorm(t, gain)
    return jnp.concatenate([t[..., :NOPE_DIM], apply_rope(t[..., NOPE_DIM:], cos, sin)], axis=-1)


def shared_latent_kv(h, kv_norm, kv_w_down, kv_latent_norm, kv_w_up, kv_k_norm, cos, sin):
    B_, S_, _ = h.shape
    hn = rmsnorm(h, kv_norm)
    ckr = hn @ kv_w_down
    c_kv = rmsnorm(ckr[..., :KV_RANK], kv_latent_norm)
    k_rope = ckr[..., KV_RANK:]
    kv = (c_kv @ kv_w_up).reshape(B_, S_, B_HEADS, NOPE_DIM + V_DIM)
    k_nope, v = kv[..., :NOPE_DIM], kv[..., NOPE_DIM:]
    k = jnp.concatenate([k_nope, jnp.broadcast_to(k_rope[:, :, None, :], (B_, S_, B_HEADS, ROPE_DIM))], axis=-1)
    k = qk_norm_rope(k, kv_k_norm, cos, sin)
    return k, v


def mla_mixer(h, norm_g, w_dq, q_latent_norm, w_uq, q_gain, w_o, k, v, cos, sin):
    B_, S_, _ = h.shape
    hn = rmsnorm(h, norm_g)
    cq = rmsnorm(hn @ w_dq, q_latent_norm)
    q = (cq @ w_uq).reshape(B_, S_, B_HEADS, QK_DIM)
    q = qk_norm_rope(q, q_gain, cos, sin)
    nb = S_ // MLA_BLOCK
    qb = jnp.transpose(q.reshape(B_, nb, MLA_BLOCK, B_HEADS, QK_DIM), (1, 0, 2, 3, 4))
    starts = jnp.arange(nb, dtype=jnp.int32) * MLA_BLOCK
    key_pos = jnp.arange(S_, dtype=jnp.int32)
    scale = QK_DIM ** -0.5

    def one_block(args):
        qblk, s0 = args
        logits = jnp.einsum('bqhd,bkhd->bhqk', qblk, k).astype(jnp.float32) * scale
        qpos = s0 + jnp.arange(MLA_BLOCK, dtype=jnp.int32)
        causal = key_pos[None, :] <= qpos[:, None]
        p = jax.nn.softmax(jnp.where(causal, logits, -jnp.inf), axis=-1)
        return jnp.einsum('bhqk,bkhd->bqhd', p.astype(v.dtype), v)

    out = lax.map(one_block, (qb, starts))
    out = jnp.transpose(out, (1, 0, 2, 3, 4)).reshape(B_, S_, B_HEADS * V_DIM)
    return (out @ w_o).astype(h.dtype)


def conv_ffn(h, norm_g, w_up, conv_w, conv_b, w_down):
    hn = rmsnorm(h, norm_g)
    u = hn @ w_up
    u = lax.conv_general_dilated(
        u, conv_w[:, None, :].astype(u.dtype), window_strides=(1,),
        padding=[(CONV_WIDTH - 1, 0)], dimension_numbers=('NWC', 'WIO', 'NWC'),
        feature_group_count=u.shape[-1]) + conv_b.astype(u.dtype)
    gate, up = jnp.split(u, 2, axis=-1)
    return ((jax.nn.silu(gate) * up) @ w_down).astype(h.dtype)


def setup_inputs(seed: int = 0) -> dict:
    key = jax.random.key(seed)
    ks = jax.random.split(key, 26)
    f32 = jnp.float32

    def w(k, shape, fan_in):
        return jax.random.normal(k, shape, f32) * (fan_in ** -0.5)

    def gain(k, shape):
        return 1.0 + 0.02 * jax.random.normal(k, shape, f32)

    return {
        "x": jax.random.normal(ks[0], (BATCH, SEQ, D_MODEL), f32),
        "a_attn_norm": gain(ks[1], (N_A_LAYERS, D_MODEL)),
        "a_w_qkv": w(ks[2], (N_A_LAYERS, D_MODEL, A_QKV_WIDTH), D_MODEL),
        "a_q_norm": gain(ks[3], (N_A_LAYERS, N_A_GROUPS, A_HEAD_DIM)),
        "a_k_norm": gain(ks[4], (N_A_LAYERS, N_A_GROUPS, A_HEAD_DIM)),
        "a_w_o": w(ks[5], (N_A_LAYERS, A_OUT_WIDTH, D_MODEL), A_OUT_WIDTH),
        "rel_bias": 0.5 * jax.random.normal(ks[6], (N_BUCKETS, N_A_GROUPS, A_HEADS_PER_GROUP), f32),
        "kv_norm": gain(ks[7], (D_MODEL,)),
        "kv_w_down": w(ks[8], (D_MODEL, KV_RANK + ROPE_DIM), D_MODEL),
        "kv_latent_norm": gain(ks[9], (KV_RANK,)),
        "kv_w_up": w(ks[10], (KV_RANK, B_HEADS * (NOPE_DIM + V_DIM)), KV_RANK),
        "kv_k_norm": gain(ks[11], (QK_DIM,)),
        "b_attn_norm": gain(ks[12], (N_B_LAYERS, D_MODEL)),
        "b_w_dq": w(ks[13], (N_B_LAYERS, D_MODEL, Q_RANK), D_MODEL),
        "b_q_latent_norm": gain(ks[14], (N_B_LAYERS, Q_RANK)),
        "b_w_uq": w(ks[15], (N_B_LAYERS, Q_RANK, B_HEADS * QK_DIM), Q_RANK),
        "b_q_norm": gain(ks[16], (N_B_LAYERS, QK_DIM)),
        "b_w_o": w(ks[17], (N_B_LAYERS, B_HEADS * V_DIM, D_MODEL), B_HEADS * V_DIM),
        "ffn_norm": gain(ks[18], (DEPTH, D_MODEL)),
        "ffn_w_up": w(ks[19], (DEPTH, D_MODEL, 2 * D_FF), D_MODEL),
        "ffn_conv_w": w(ks[20], (DEPTH, CONV_WIDTH, 2 * D_FF), CONV_WIDTH),
        "ffn_conv_b": 0.01 * jax.random.normal(ks[21], (DEPTH, 2 * D_FF), f32),
        "ffn_w_down": w(ks[22], (DEPTH, D_FF, D_MODEL), D_FF),
    }


def reference(x, a_attn_norm, a_w_qkv, a_q_norm, a_k_norm, a_w_o, rel_bias,
              kv_norm, kv_w_down, kv_latent_norm, kv_w_up, kv_k_norm,
              b_attn_norm, b_w_dq, b_q_latent_norm, b_w_uq, b_q_norm, b_w_o,
              ffn_norm, ffn_w_up, ffn_conv_w, ffn_conv_b, ffn_w_down):
    S_ = x.shape[1]
    cos, sin = rope_tables(S_)
    h = x
    k_shared = None
    v_shared = None
    for layer in range(DEPTH):
        if layer < N_A_LAYERS:
            h = h + dilated_mixer(h, a_attn_norm[layer], a_w_qkv[layer], a_q_norm[layer],
                                  a_k_norm[layer], a_w_o[layer], rel_bias)
        else:
            if layer == N_A_LAYERS:
                k_shared, v_shared = shared_latent_kv(h, kv_norm, kv_w_down, kv_latent_norm,
                                                      kv_w_up, kv_k_norm, cos, sin)
            i = layer - N_A_LAYERS
            h = h + mla_mixer(h, b_attn_norm[i], b_w_dq[i], b_q_latent_norm[i], b_w_uq[i],
                              b_q_norm[i], b_w_o[i], k_shared, v_shared, cos, sin)
        h = h + conv_ffn(h, ffn_norm[layer], ffn_w_up[layer], ffn_conv_w[layer],
                         ffn_conv_b[layer], ffn_w_down[layer])
    return h
```

```python
import functools
import math

import numpy as np
import jax
import jax.numpy as jnp
from jax import lax
from jax.experimental import pallas as pl
from jax.experimental.pallas import tpu as pltpu

F32 = jnp.float32
BF16 = jnp.bfloat16

EPS = 1e-6
A_GROUPS = ((128, 1), (512, 4), (2048, 16))
N_BUCKETS = 32
REL_MAX_EXACT = N_BUCKETS // 2
REL_MAX_DISTANCE = 2048
NOPE_DIM = 128
ROPE_DIM = 64
V_DIM = 128
QK_DIM = NOPE_DIM + ROPE_DIM
ROPE_THETA = 10000.0
CONV_WIDTH = 3

LANES = 128
BF16_SUBLANES = 16
V7X_VMEM_BYTES = 64 * 1024 * 1024
HALO = BF16_SUBLANES
NEG = -1e30


def _pick(dim, pref, align):
    best = None
    for cand in range(align, min(dim, pref) + 1, align):
        if dim % cand == 0:
            best = cand
    assert best is not None, (dim, pref, align)
    return best


def _params(semantics, vmem_bytes):
    limit = min(int(vmem_bytes * 1.25) + (4 << 20), V7X_VMEM_BYTES - (6 << 20))
    return pltpu.CompilerParams(dimension_semantics=semantics, vmem_limit_bytes=limit)


def _dot(a, b):
    return jnp.dot(a, b, preferred_element_type=F32)


def _dot_nt(a, b):
    return lax.dot_general(a, b, (((1,), (1,)), ((), ())), preferred_element_type=F32)


def _rms_scale(t, dim):
    return lax.rsqrt(jnp.sum(t * t, axis=-1, keepdims=True) / dim + EPS)


def _rmsnorm_body(x_ref, g_ref, *o_refs):
    x = x_ref[...]
    y = x * _rms_scale(x, x.shape[-1])
    for i, o_ref in enumerate(o_refs):
        o_ref[...] = (y * g_ref[i:i + 1, :]).astype(o_ref.dtype)


def _rmsnorm(x, gains):
    M, D = x.shape
    n = gains.shape[0]
    br = _pick(M, 256, 8)
    vmem = 2 * br * D * 4 + 2 * n * br * D * 2 + 2 * br * D * 4
    outs = pl.pallas_call(
        _rmsnorm_body,
        grid=(M // br,),
        in_specs=[pl.BlockSpec((br, D), lambda i: (i, 0)),
                  pl.BlockSpec((n, D), lambda i: (0, 0))],
        out_specs=[pl.BlockSpec((br, D), lambda i: (i, 0))] * n,
        out_shape=[jax.ShapeDtypeStruct((M, D), BF16)] * n,
        compiler_params=_params(("parallel",), vmem),
        name="rmsnorm",
    )(x, gains)
    return outs


def _qkv_body(x_ref, w_ref, g_ref, o_ref, *, n_norm_blocks, heads_per_blk):
    acc = _dot(x_ref[...], w_ref[...])
    n = pl.program_id(1)

    @pl.when(n < n_norm_blocks)
    def _():
        for j in range(heads_per_blk):
            t = acc[:, j * LANES:(j + 1) * LANES]
            t = t * _rms_scale(t, LANES) * g_ref[j:j + 1, :]
            o_ref[j] = t.astype(o_ref.dtype)

    @pl.when(n >= n_norm_blocks)
    def _():
        for j in range(heads_per_blk):
            o_ref[j] = acc[:, j * LANES:(j + 1) * LANES].astype(o_ref.dtype)


def _qkv_proj(hn, w, head_gains):
    T, D = hn.shape
    N = w.shape[1]
    n_slots = N // LANES
    bm = _pick(T, 1024, BF16_SUBLANES)
    bn = _pick(N // 3, 1024, LANES)
    hpb = bn // LANES
    vmem = 2 * bm * D * 2 + 2 * D * bn * 2 + 2 * bm * bn * 2 + 2 * bm * bn * 4
    body = functools.partial(_qkv_body, n_norm_blocks=2 * (N // 3) // bn, heads_per_blk=hpb)
    return pl.pallas_call(
        body,
        grid=(T // bm, N // bn),
        in_specs=[pl.BlockSpec((bm, D), lambda m, n: (m, 0)),
                  pl.BlockSpec((D, bn), lambda m, n: (0, n)),
                  pl.BlockSpec((hpb, LANES), lambda m, n: (n, 0))],
        out_specs=pl.BlockSpec((hpb, bm, LANES), lambda m, n: (n, m, 0)),
        out_shape=jax.ShapeDtypeStruct((n_slots, T, LANES), BF16),
        compiler_params=_params(("parallel", "parallel"), vmem),
        name="qkv_proj",
    )(hn, w, head_gains)


def _t5_bucket_np(dist):
    dist = np.asarray(dist, np.int64)
    d_f = np.maximum(dist, 1).astype(np.float64)
    large = REL_MAX_EXACT + (np.log(d_f / REL_MAX_EXACT) / math.log(REL_MAX_DISTANCE / REL_MAX_EXACT)
                             * (N_BUCKETS - REL_MAX_EXACT)).astype(np.int64)
    large = np.minimum(large, N_BUCKETS - 1)
    return np.where(dist < REL_MAX_EXACT, dist, large)


def _band_buckets():
    out = []
    for window, dilation in A_GROUPS:
        C = window // dilation
        qi = np.arange(C)[:, None]
        kk = np.arange(2 * C)[None, :]
        steps = qi + C - kk
        band = (steps >= 0) & (steps <= C)
        bucket = _t5_bucket_np(np.maximum(steps, 0) * dilation)
        out.append(np.where(band, bucket, -1).astype(np.int32))
    return np.stack(out)


def _dilated_body(tab_ref, bkt_ref, *refs, n_heads, seq):
    n_g = len(A_GROUPS)
    qkv_refs = refs[:3 * n_g]
    o_refs = refs[3 * n_g:4 * n_g]
    l_refs = refs[4 * n_g:5 * n_g]
    h = pl.program_id(1)
    scale = LANES ** -0.5

    for g, (window, dil) in enumerate(A_GROUPS):
        C = window // dil
        q_ref, k_ref, v_ref = qkv_refs[3 * g:3 * g + 3]
        o_ref, l_ref = o_refs[g], l_refs[g]
        n_blocks = seq // dil // C

        bkt = bkt_ref[g]
        bias = jnp.full((C, 2 * C), NEG, F32)
        for b in range(N_BUCKETS):
            bias = jnp.where(bkt == b, tab_ref[(g * n_heads + h) * N_BUCKETS + b], bias)

        def block(q, kk, vv, bias_blk):
            s = _dot_nt(q, kk) * scale + bias_blk
            m = jnp.max(s, axis=-1, keepdims=True)
            p = jnp.exp(s - m)
            l = jnp.sum(p, axis=-1, keepdims=True)
            o = _dot(p.astype(BF16), vv) / l
            return o, m + jnp.log(l)

        for r in range(dil):
            lanes = slice(r * LANES, (r + 1) * LANES)

            o, lse = block(q_ref[0:C, lanes], k_ref[0:C, lanes], v_ref[0:C, lanes], bias[:, C:])
            o_ref[0:C, lanes] = o.astype(o_ref.dtype)
            l_ref[0:C, lanes] = jnp.broadcast_to(lse, (C, LANES))

            def body(c, carry, lanes=lanes, q_ref=q_ref, k_ref=k_ref, v_ref=v_ref,
                     o_ref=o_ref, l_ref=l_ref, bias=bias, C=C):
                r0 = pl.multiple_of(c * C, C)
                rp = pl.multiple_of((c - 1) * C, C)
                o, lse = block(q_ref[pl.ds(r0, C), lanes], k_ref[pl.ds(rp, 2 * C), lanes],
                               v_ref[pl.ds(rp, 2 * C), lanes], bias)
                o_ref[pl.ds(r0, C), lanes] = o.astype(o_ref.dtype)
                l_ref[pl.ds(r0, C), lanes] = jnp.broadcast_to(lse, (C, LANES))
                return carry

            lax.fori_loop(1, n_blocks, body, 0)


def _dilated_attention(qkv, rel_bias, batch, seq, n_heads):
    n_g = len(A_GROUPS)
    tab = jnp.transpose(rel_bias, (1, 2, 0)).reshape(-1).astype(F32)
    bkt = jnp.asarray(_band_buckets())
    C = A_GROUPS[0][0] // A_GROUPS[0][1]

    in_specs = [pl.BlockSpec(memory_space=pltpu.SMEM),
                pl.BlockSpec(bkt.shape, lambda b, h: (0, 0, 0))]
    args = [tab, bkt]
    out_specs, out_shape = [], []
    vmem = 0
    for g, (window, dil) in enumerate(A_GROUPS):
        assert window // dil == C and seq % (dil * C) == 0
        n = seq // dil
        view = qkv.reshape(qkv.shape[0], batch, n, dil * LANES)
        for s in range(3):
            slot0 = (s * n_g + g) * n_heads
            in_specs.append(pl.BlockSpec((None, None, n, dil * LANES),
                                         lambda b, h, slot0=slot0: (slot0 + h, b, 0, 0)))
            args.append(view)
        vmem += 3 * 2 * seq * LANES * 2
    for dtype in (BF16, F32):
        for window, dil in A_GROUPS:
            n = seq // dil
            out_specs.append(pl.BlockSpec((None, None, n, dil * LANES), lambda b, h: (h, b, 0, 0)))
            out_shape.append(jax.ShapeDtypeStruct((n_heads, batch, n, dil * LANES), dtype))
            vmem += 2 * seq * LANES * jnp.dtype(dtype).itemsize
    outs = pl.pallas_call(
        functools.partial(_dilated_body, n_heads=n_heads, seq=seq),
        grid=(batch, n_heads),
        in_specs=in_specs,
        out_specs=out_specs,
        out_shape=out_shape,
        compiler_params=_params(("parallel", "parallel"), vmem),
        name="dilated_attention",
    )(*args)
    outs = [o.reshape(n_heads, batch * seq, LANES) for o in outs]
    return outs[:n_g], outs[n_g:]


def _merge_wo_body(*refs, n_heads):
    n_g = len(A_GROUPS)
    o_refs = refs[:n_g]
    l_refs = refs[n_g:2 * n_g]
    w_ref, res_ref, out_ref, a_scr = refs[2 * n_g:]

    @pl.when(pl.program_id(1) == 0)
    def _():
        for h in range(n_heads):
            lses = [l_ref[h] for l_ref in l_refs]
            mx = functools.reduce(jnp.maximum, lses)
            es = [jnp.exp(l - mx) for l in lses]
            den = functools.reduce(lambda a, b: a + b, es)
            num = functools.reduce(lambda a, b: a + b,
                                   [e * o_ref[h].astype(F32) for e, o_ref in zip(es, o_refs)])
            a_scr[:, h * LANES:(h + 1) * LANES] = (num / den).astype(a_scr.dtype)

    out_ref[...] = res_ref[...] + _dot(a_scr[...], w_ref[...])


def _merge_wo(os_, lses, w, res):
    n_heads, T, _ = os_[0].shape
    K, N = w.shape
    bm = _pick(T, 256, BF16_SUBLANES)
    bn = _pick(N, 1024, LANES)
    n_g = len(A_GROUPS)
    vmem = (n_g * 2 * n_heads * bm * LANES * (2 + 4) + 2 * K * bn * 2 + 4 * bm * bn * 4 + bm * K * 2)
    o_spec = pl.BlockSpec((n_heads, bm, LANES), lambda m, n: (0, m, 0))
    return pl.pallas_call(
        functools.partial(_merge_wo_body, n_heads=n_heads),
        grid=(T // bm, N // bn),
        in_specs=[o_spec] * (2 * n_g) + [pl.BlockSpec((K, bn), lambda m, n: (0, n)),
                                          pl.BlockSpec((bm, bn), lambda m, n: (m, n))],
        out_specs=pl.BlockSpec((bm, bn), lambda m, n: (m, n)),
        out_shape=jax.ShapeDtypeStruct((T, N), F32),
        scratch_shapes=[pltpu.VMEM((bm, K), BF16)],
        compiler_params=_params(("parallel", "arbitrary"), vmem),
        name="merge_wo",
    )(*os_, *lses, w, res)


def _mm_res_body(x_ref, w_ref, res_ref, out_ref):
    out_ref[...] = res_ref[...] + _dot(x_ref[...], w_ref[...])


def _mm_res(x, w, res, bm_pref, bn_pref):
    M, K = x.shape
    N = w.shape[1]
    bm = _pick(M, bm_pref, BF16_SUBLANES)
    bn = _pick(N, bn_pref, LANES)
    vmem = 2 * bm * K * 2 + 2 * K * bn * 2 + 5 * bm * bn * 4
    return pl.pallas_call(
        _mm_res_body,
        grid=(M // bm, N // bn),
        in_specs=[pl.BlockSpec((bm, K), lambda m, n: (m, 0)),
                  pl.BlockSpec((K, bn), lambda m, n: (0, n)),
                  pl.BlockSpec((bm, bn), lambda m, n: (m, n))],
        out_specs=pl.BlockSpec((bm, bn), lambda m, n: (m, n)),
        out_shape=jax.ShapeDtypeStruct((M, N), F32),
        compiler_params=_params(("parallel", "parallel"), vmem),
        name="mm_res",
    )(x, w, res)


def _ffn_up_body(x_ref, halo_ref, wg_ref, wu_ref, cwg_ref, cwu_ref, cbg_ref, cbu_ref, o_ref, lhs_scr,
                 *, blocks_per_seq):
    m = pl.program_id(0)
    n = pl.program_id(1)
    first_of_seq = (m % blocks_per_seq) == 0

    @pl.when(n == 0)
    def _():
        lhs_scr[HALO:, :] = x_ref[...]

    @pl.when((n == 0) & first_of_seq)
    def _():
        lhs_scr[:HALO, :] = jnp.zeros((HALO, lhs_scr.shape[1]), lhs_scr.dtype)

    @pl.when((n == 0) & jnp.logical_not(first_of_seq))
    def _():
        lhs_scr[:HALO, :] = halo_ref[...]

    lhs = lhs_scr[...]

    def conv(acc, cw_ref, cb_ref):
        a1 = pltpu.roll(acc, 1, 0)
        a2 = pltpu.roll(acc, 2, 0)
        return (acc[HALO:] * cw_ref[2:3, :] + a1[HALO:] * cw_ref[1:2, :]
                + a2[HALO:] * cw_ref[0:1, :] + cb_ref[...])

    gate = conv(_dot(lhs, wg_ref[...]), cwg_ref, cbg_ref)
    up = conv(_dot(lhs, wu_ref[...]), cwu_ref, cbu_ref)
    o_ref[...] = (gate / (1.0 + jnp.exp(-gate)) * up).astype(o_ref.dtype)


def _ffn_up(hn, w_up, conv_w, conv_b, seq):
    T, D = hn.shape
    F = w_up.shape[1] // 2
    bm = _pick(seq, 1024, BF16_SUBLANES)
    bn = _pick(F, 256, LANES)
    nb = F // bn
    conv_b2 = conv_b.reshape(1, 2 * F)
    vmem = (2 * bm * D * 2 + (bm + HALO) * D * 2 + 4 * D * bn * 2 + 2 * bm * bn * 2
            + 8 * (bm + HALO) * bn * 4)
    body = functools.partial(_ffn_up_body, blocks_per_seq=seq // bm)
    halo_blocks = bm // HALO
    return pl.pallas_call(
        body,
        grid=(T // bm, nb),
        in_specs=[pl.BlockSpec((bm, D), lambda m, n: (m, 0)),
                  pl.BlockSpec((HALO, D), lambda m, n: (jnp.maximum(m * halo_blocks - 1, 0), 0)),
                  pl.BlockSpec((D, bn), lambda m, n: (0, n)),
                  pl.BlockSpec((D, bn), lambda m, n: (0, n + nb)),
                  pl.BlockSpec((CONV_WIDTH, bn), lambda m, n: (0, n)),
                  pl.BlockSpec((CONV_WIDTH, bn), lambda m, n: (0, n + nb)),
                  pl.BlockSpec((1, bn), lambda m, n: (0, n)),
                  pl.BlockSpec((1, bn), lambda m, n: (0, n + nb))],
        out_specs=pl.BlockSpec((bm, bn), lambda m, n: (m, n)),
        out_shape=jax.ShapeDtypeStruct((T, F), BF16),
        scratch_shapes=[pltpu.VMEM((bm + HALO, D), BF16)],
        compiler_params=_params(("parallel", "arbitrary"), vmem),
        name="ffn_up",
    )(hn, hn, w_up, w_up, conv_w, conv_w, conv_b2, conv_b2)


def _rope_tables(seq):
    half = ROPE_DIM // 2
    pos = jnp.arange(seq, dtype=F32)
    inv = ROPE_THETA ** (-jnp.arange(0, ROPE_DIM, 2, dtype=F32) / ROPE_DIM)
    ang = pos[:, None] * inv[None, :]
    cos, sin = jnp.cos(ang), jnp.sin(ang)
    zeros = jnp.zeros((seq, LANES - ROPE_DIM), F32)
    z_half = jnp.zeros((seq, half), F32)
    c_tab = jnp.concatenate([cos, cos, zeros], axis=-1)
    s_lo = jnp.concatenate([-sin, z_half, zeros], axis=-1)
    s_hi = jnp.concatenate([z_half, sin, zeros], axis=-1)
    return c_tab, s_lo, s_hi


def _rope(x, c_tab, s_lo, s_hi):
    half = ROPE_DIM // 2
    return (x * c_tab + pltpu.roll(x, LANES - half, 1) * s_lo + pltpu.roll(x, half, 1) * s_hi)


def _kv_down_body(x_ref, w_ref, gl_ref, gr_ref, c_ref, slo_ref, shi_ref, ckv_ref, kr_ref, ssq_ref, *, kv_rank):
    acc = _dot(x_ref[...], w_ref[...])
    c = acc[:, :kv_rank]
    ckv_ref[...] = (c * _rms_scale(c, kv_rank) * gl_ref[...]).astype(ckv_ref.dtype)
    kr = acc[:, kv_rank:kv_rank + LANES]
    ssq_ref[...] = jnp.broadcast_to(jnp.sum(kr * kr, axis=-1, keepdims=True), kr.shape)
    kr_ref[...] = _rope(kr * gr_ref[...], c_ref[...], slo_ref[...], shi_ref[...])


def _kv_down(hn, w_pad, latent_gain, rope_gain_pad, tables, seq, kv_rank):
    T, D = hn.shape
    N = w_pad.shape[1]
    bm = _pick(seq, 1024, BF16_SUBLANES)
    spb = seq // bm
    vmem = 2 * bm * D * 2 + 2 * D * N * 2 + 4 * bm * N * 4 + 2 * bm * kv_rank * 2 + 10 * bm * LANES * 4
    tab_spec = pl.BlockSpec((bm, LANES), lambda m: (m % spb, 0))
    return pl.pallas_call(
        functools.partial(_kv_down_body, kv_rank=kv_rank),
        grid=(T // bm,),
        in_specs=[pl.BlockSpec((bm, D), lambda m: (m, 0)),
                  pl.BlockSpec((D, N), lambda m: (0, 0)),
                  pl.BlockSpec((1, kv_rank), lambda m: (0, 0)),
                  pl.BlockSpec((1, LANES), lambda m: (0, 0)),
                  tab_spec, tab_spec, tab_spec],
        out_specs=[pl.BlockSpec((bm, kv_rank), lambda m: (m, 0)),
                   pl.BlockSpec((bm, LANES), lambda m: (m, 0)),
                   pl.BlockSpec((bm, LANES), lambda m: (m, 0))],
        out_shape=[jax.ShapeDtypeStruct((T, kv_rank), BF16),
                   jax.ShapeDtypeStruct((T, LANES), F32),
                   jax.ShapeDtypeStruct((T, LANES), F32)],
        compiler_params=_params(("parallel",), vmem),
        name="kv_down",
    )(hn, w_pad, latent_gain, rope_gain_pad, *tables)


def _kv_up_body(c_ref, w_ref, kr_ref, ssq_ref, gn_ref, k_ref, v_ref, *, heads_per_blk):
    acc = _dot(c_ref[...], w_ref[...])
    kr = kr_ref[...]
    ssq = ssq_ref[...]
    width = NOPE_DIM + V_DIM
    for j in range(heads_per_blk):
        kn = acc[:, j * width:j * width + NOPE_DIM]
        f = lax.rsqrt((jnp.sum(kn * kn, axis=-1, keepdims=True) + ssq) / QK_DIM + EPS)
        k_ref[j, :, 0:NOPE_DIM] = (kn * f * gn_ref[...]).astype(k_ref.dtype)
        k_ref[j, :, NOPE_DIM:2 * LANES] = (kr * f).astype(k_ref.dtype)
        v_ref[j] = acc[:, j * width + NOPE_DIM:(j + 1) * width].astype(v_ref.dtype)


def _kv_up(ckv, w, kr, ssq, nope_gain, n_heads):
    T, K = ckv.shape
    width = NOPE_DIM + V_DIM
    bm = _pick(T, 1024, BF16_SUBLANES)
    hpb = _pick(n_heads, 4, 1)
    bn = hpb * width
    vmem = 2 * bm * K * 2 + 2 * K * bn * 2 + 4 * bm * LANES * 4 + 2 * bm * bn * 4 + 2 * hpb * bm * 3 * LANES * 2
    return pl.pallas_call(
        functools.partial(_kv_up_body, heads_per_blk=hpb),
        grid=(T // bm, n_heads // hpb),
        in_specs=[pl.BlockSpec((bm, K), lambda m, n: (m, 0)),
                  pl.BlockSpec((K, bn), lambda m, n: (0, n)),
                  pl.BlockSpec((bm, LANES), lambda m, n: (m, 0)),
                  pl.BlockSpec((bm, LANES), lambda m, n: (m, 0)),
                  pl.BlockSpec((1, NOPE_DIM), lambda m, n: (0, 0))],
        out_specs=[pl.BlockSpec((hpb, bm, 2 * LANES), lambda m, n: (n, m, 0)),
                   pl.BlockSpec((hpb, bm, V_DIM), lambda m, n: (n, m, 0))],
        out_shape=[jax.ShapeDtypeStruct((n_heads, T, 2 * LANES), BF16),
                   jax.ShapeDtypeStruct((n_heads, T, V_DIM), BF16)],
        compiler_params=_params(("parallel", "parallel"), vmem),
        name="kv_up",
    )(ckv, w, kr, ssq, nope_gain)


def _q_down_body(x_ref, w_ref, g_ref, o_ref):
    acc = _dot(x_ref[...], w_ref[...])
    o_ref[...] = (acc * _rms_scale(acc, acc.shape[-1]) * g_ref[...]).astype(o_ref.dtype)


def _q_down(hn, w, gain):
    T, D = hn.shape
    N = w.shape[1]
    bm = _pick(T, 1024, BF16_SUBLANES)
    vmem = 2 * bm * D * 2 + 2 * D * N * 2 + 3 * bm * N * 4 + 2 * bm * N * 2
    return pl.pallas_call(
        _q_down_body,
        grid=(T // bm,),
        in_specs=[pl.BlockSpec((bm, D), lambda m: (m, 0)),
                  pl.BlockSpec((D, N), lambda m: (0, 0)),
                  pl.BlockSpec((1, N), lambda m: (0, 0))],
        out_specs=pl.BlockSpec((bm, N), lambda m: (m, 0)),
        out_shape=jax.ShapeDtypeStruct((T, N), BF16),
        compiler_params=_params(("parallel",), vmem),
        name="q_down",
    )(hn, w, gain)


def _q_up_body(c_ref, w_ref, gn_ref, gr_ref, ct_ref, slo_ref, shi_ref, q_ref, *, heads_per_blk):
    acc = _dot(c_ref[...], w_ref[...])
    ct, slo, shi = ct_ref[...], slo_ref[...], shi_ref[...]
    for j in range(heads_per_blk):
        tn = acc[:, 2 * j * LANES:(2 * j + 1) * LANES]
        tr = acc[:, (2 * j + 1) * LANES:(2 * j + 2) * LANES]
        ssq = jnp.sum(tn * tn, axis=-1, keepdims=True) + jnp.sum(tr * tr, axis=-1, keepdims=True)
        f = lax.rsqrt(ssq / QK_DIM + EPS)
        q_ref[j, :, 0:LANES] = (tn * f * gn_ref[...]).astype(q_ref.dtype)
        q_ref[j, :, LANES:2 * LANES] = (_rope(tr * gr_ref[...], ct, slo, shi) * f).astype(q_ref.dtype)


def _q_up(cq, w_pad, nope_gain, rope_gain_pad, tables, seq, n_heads):
    T, K = cq.shape
    bm = _pick(seq, 1024, BF16_SUBLANES)
    spb = seq // bm
    hpb = _pick(n_heads, 4, 1)
    bn = hpb * 2 * LANES
    vmem = 2 * bm * K * 2 + 2 * K * bn * 2 + 6 * bm * LANES * 4 + 3 * bm * bn * 4 + 2 * bm * bn * 2
    tab_spec = pl.BlockSpec((bm, LANES), lambda m, n: (m % spb, 0))
    return pl.pallas_call(
        functools.partial(_q_up_body, heads_per_blk=hpb),
        grid=(T // bm, n_heads // hpb),
        in_specs=[pl.BlockSpec((bm, K), lambda m, n: (m, 0)),
                  pl.BlockSpec((K, bn), lambda m, n: (0, n)),
                  pl.BlockSpec((1, LANES), lambda m, n: (0, 0)),
                  pl.BlockSpec((1, LANES), lambda m, n: (0, 0)),
                  tab_spec, tab_spec, tab_spec],
        out_specs=pl.BlockSpec((hpb, bm, 2 * LANES), lambda m, n: (n, m, 0)),
        out_shape=jax.ShapeDtypeStruct((n_heads, T, 2 * LANES), BF16),
        compiler_params=_params(("parallel", "parallel"), vmem),
        name="q_up",
    )(cq, w_pad, nope_gain, rope_gain_pad, *tables)


def _mla_body(q_ref, k_ref, v_ref, o_ref, *, blk):
    qi = pl.program_id(2)
    q = q_ref[...]
    scale = QK_DIM ** -0.5

    def chunk(j, carry, diagonal):
        m, l, acc = carry
        k0 = pl.multiple_of(j * blk, blk)
        s = _dot_nt(q, k_ref[pl.ds(k0, blk), :]) * scale
        if diagonal:
            row = lax.broadcasted_iota(jnp.int32, s.shape, 0)
            col = lax.broadcasted_iota(jnp.int32, s.shape, 1)
            s = jnp.where(col <= row, s, NEG)
        m_new = jnp.maximum(m, jnp.max(s, axis=-1, keepdims=True))
        alpha = jnp.exp(m - m_new)
        p = jnp.exp(s - m_new)
        l = alpha * l + jnp.sum(p, axis=-1, keepdims=True)
        acc = alpha * acc + _dot(p.astype(BF16), v_ref[pl.ds(k0, blk), :])
        return m_new, l, acc

    carry = (jnp.full((blk, 1), NEG, F32), jnp.zeros((blk, 1), F32), jnp.zeros((blk, V_DIM), F32))
    carry = lax.fori_loop(0, qi, lambda j, c: chunk(j, c, False), carry)
    _, l, acc = chunk(qi, carry, True)
    o_ref[...] = (acc / l).astype(o_ref.dtype)


def _mla_attention(q, k, v, batch, seq):
    n_heads, T, _ = q.shape
    blk = _pick(seq, 512, BF16_SUBLANES)
    nq = seq // blk
    vmem = 2 * blk * 2 * LANES * 2 + 2 * seq * 3 * LANES * 2 + 2 * blk * V_DIM * 2 + 6 * blk * blk * 4
    return pl.pallas_call(
        functools.partial(_mla_body, blk=blk),
        grid=(batch, n_heads, nq),
        in_specs=[pl.BlockSpec((None, blk, 2 * LANES), lambda b, h, i: (h, b * nq + i, 0)),
                  pl.BlockSpec((None, seq, 2 * LANES), lambda b, h, i: (h, b, 0)),
                  pl.BlockSpec((None, seq, V_DIM), lambda b, h, i: (h, b, 0))],
        out_specs=pl.BlockSpec((blk, V_DIM), lambda b, h, i: (b * nq + i, h)),
        out_shape=jax.ShapeDtypeStruct((T, n_heads * V_DIM), BF16),
        compiler_params=_params(("parallel", "parallel", "parallel"), vmem),
        name="mla_attention",
    )(q, k, v)


def _pad_lanes(g, width):
    return jnp.pad(g, (0, width - g.shape[0])).reshape(1, width)


def _conv_ffn(h, gain, w_up, conv_w, conv_b, w_down, seq):
    (hn,) = _rmsnorm(h, gain.reshape(1, -1))
    act = _ffn_up(hn, w_up.astype(BF16), conv_w, conv_b, seq)
    return _mm_res(act, w_down.astype(BF16), h, 512, 512)


def kernel(x, a_attn_norm, a_w_qkv, a_q_norm, a_k_norm, a_w_o, rel_bias, kv_norm, kv_w_down,
           kv_latent_norm, kv_w_up, kv_k_norm, b_attn_norm, b_w_dq, b_q_latent_norm, b_w_uq, b_q_norm,
           b_w_o, ffn_norm, ffn_w_up, ffn_conv_w, ffn_conv_b, ffn_w_down):
    batch, seq, d_model = x.shape
    T = batch * seq
    n_g = len(A_GROUPS)
    n_a_layers = a_w_qkv.shape[0]
    n_b_layers = b_w_dq.shape[0]
    a_heads = a_w_qkv.shape[2] // (3 * n_g * LANES)
    b_heads = b_w_o.shape[1] // V_DIM
    kv_rank = kv_w_down.shape[1] - ROPE_DIM
    tables = _rope_tables(seq)

    h = x.reshape(T, d_model)
    k_shared = v_shared = None
    for layer in range(n_a_layers + n_b_layers):
        if layer < n_a_layers:
            (hn,) = _rmsnorm(h, a_attn_norm[layer].reshape(1, -1))
            ones = jnp.ones((n_g, LANES), F32)
            head_gains = jnp.repeat(jnp.concatenate([a_q_norm[layer], a_k_norm[layer], ones], axis=0),
                                    a_heads, axis=0)
            qkv = _qkv_proj(hn, a_w_qkv[layer].astype(BF16), head_gains)
            os_, lses = _dilated_attention(qkv, rel_bias, batch, seq, a_heads)
            h = _merge_wo(os_, lses, a_w_o[layer].astype(BF16), h)
        else:
            i = layer - n_a_layers
            if layer == n_a_layers:
                hn_kv, hn_q = _rmsnorm(h, jnp.stack([kv_norm, b_attn_norm[i]]))
                w_kvd = jnp.pad(kv_w_down, ((0, 0), (0, LANES - ROPE_DIM))).astype(BF16)
                ckv, kr, ssq = _kv_down(hn_kv, w_kvd, kv_latent_norm.reshape(1, -1),
                                        _pad_lanes(kv_k_norm[NOPE_DIM:], LANES), tables, seq, kv_rank)
                k_shared, v_shared = _kv_up(ckv, kv_w_up.astype(BF16), kr, ssq,
                                            kv_k_norm[:NOPE_DIM].reshape(1, -1), b_heads)
            else:
                (hn_q,) = _rmsnorm(h, b_attn_norm[i].reshape(1, -1))
            cq = _q_down(hn_q, b_w_dq[i].astype(BF16), b_q_latent_norm[i].reshape(1, -1))
            q_rank = b_w_uq.shape[1]
            w_uq = jnp.pad(b_w_uq[i].reshape(q_rank, b_heads, QK_DIM),
                           ((0, 0), (0, 0), (0, 2 * LANES - QK_DIM)))
            w_uq = w_uq.reshape(q_rank, b_heads * 2 * LANES).astype(BF16)
            q = _q_up(cq, w_uq, b_q_norm[i][:NOPE_DIM].reshape(1, -1),
                      _pad_lanes(b_q_norm[i][NOPE_DIM:], LANES), tables, seq, b_heads)
            attn = _mla_attention(q, k_shared, v_shared, batch, seq)
            h = _mm_res(attn, b_w_o[i].astype(BF16), h, 1024, 512)
        h = _conv_ffn(h, ffn_norm[layer], ffn_w_up[layer], ffn_conv_w[layer], ffn_conv_b[layer],
                      ffn_w_down[layer], seq)
    return h.reshape(batch, seq, d_model)
```

```python
import functools
import math

import numpy as np
import jax
import jax.numpy as jnp
from jax import lax
from jax.experimental import pallas as pl
from jax.experimental.pallas import tpu as pltpu

F32 = jnp.float32
BF16 = jnp.bfloat16

EPS = 1e-6
A_GROUPS = ((128, 1), (512, 4), (2048, 16))
N_BUCKETS = 32
REL_MAX_EXACT = N_BUCKETS // 2
REL_MAX_DISTANCE = 2048
NOPE_DIM = 128
ROPE_DIM = 64
V_DIM = 128
QK_DIM = NOPE_DIM + ROPE_DIM
ROPE_THETA = 10000.0
CONV_WIDTH = 3

LANES = 128
BF16_SUBLANES = 16
V7X_VMEM_BYTES = 64 * 1024 * 1024
HALO = BF16_SUBLANES
ATTN_BLOCKS_PER_ITER = 8
ROW_CHUNKS = 8
NEG = -1e30


def _pick(dim, pref, align):
    best = None
    for cand in range(align, min(dim, pref) + 1, align):
        if dim % cand == 0:
            best = cand
    assert best is not None, (dim, pref, align)
    return best


def _params(semantics, vmem_bytes):
    limit = min(int(vmem_bytes * 1.25) + (4 << 20), V7X_VMEM_BYTES - (6 << 20))
    return pltpu.CompilerParams(dimension_semantics=semantics, vmem_limit_bytes=limit)


def _dot(a, b):
    return jnp.dot(a, b, preferred_element_type=F32)


def _dot_nt(a, b):
    return lax.dot_general(a, b, (((1,), (1,)), ((), ())), preferred_element_type=F32)


def _rms_scale(t, dim):
    return lax.rsqrt(jnp.sum(t * t, axis=-1, keepdims=True) / dim + EPS)


def _rmsnorm_body(x_ref, g_ref, *o_refs):
    x = x_ref[...]
    y = x * _rms_scale(x, x.shape[-1])
    for i, o_ref in enumerate(o_refs):
        o_ref[...] = (y * g_ref[i:i + 1, :]).astype(o_ref.dtype)


def _rmsnorm(x, gains):
    M, D = x.shape
    n = gains.shape[0]
    br = _pick(M, 256, 8)
    vmem = 2 * br * D * 4 + 2 * n * br * D * 2 + 2 * br * D * 4
    outs = pl.pallas_call(
        _rmsnorm_body,
        grid=(M // br,),
        in_specs=[pl.BlockSpec((br, D), lambda i: (i, 0)),
                  pl.BlockSpec((n, D), lambda i: (0, 0))],
        out_specs=[pl.BlockSpec((br, D), lambda i: (i, 0))] * n,
        out_shape=[jax.ShapeDtypeStruct((M, D), BF16)] * n,
        compiler_params=_params(("parallel",), vmem),
        name="rmsnorm",
    )(x, gains)
    return outs


def _qkv_body(x_ref, w_ref, g_ref, o_ref, *scratch, n_norm_blocks, heads_per_blk, dil):
    acc = _dot(x_ref[...], w_ref[...])
    n = pl.program_id(1)
    rows = acc.shape[0] // dil

    def emit(normalise):
        for j in range(heads_per_blk):
            cols = slice(j * LANES, (j + 1) * LANES)
            t = acc[:, cols]
            if normalise:
                t = t * _rms_scale(t, LANES) * g_ref[j:j + 1, :]
            if dil == 1:
                o_ref[j] = t.astype(o_ref.dtype)
            else:
                scratch[0][j] = t
        if dil > 1:
            for j in range(heads_per_blk):
                for r in range(dil):
                    t = scratch[0][j, pl.ds(r, rows, stride=dil), :]
                    o_ref[j, :, r * LANES:(r + 1) * LANES] = t.astype(o_ref.dtype)

    pl.when(n < n_norm_blocks)(lambda: emit(True))
    pl.when(n >= n_norm_blocks)(lambda: emit(False))


def _qkv_proj(hn, w, head_gains, batch, seq, dil):
    T, D = hn.shape
    N = w.shape[1]
    bm = _pick(seq, 1024, BF16_SUBLANES * dil)
    bn = _pick(N // 3, 1024, LANES)
    hpb = bn // LANES
    spb = seq // bm
    vmem = 2 * bm * D * 2 + 2 * D * bn * 2 + 2 * bm * bn * 2 + 3 * bm * bn * 4
    body = functools.partial(_qkv_body, n_norm_blocks=2 * (N // 3) // bn, heads_per_blk=hpb, dil=dil)
    return pl.pallas_call(
        body,
        grid=(T // bm, N // bn),
        in_specs=[pl.BlockSpec((bm, D), lambda m, n: (m, 0)),
                  pl.BlockSpec((D, bn), lambda m, n: (0, n)),
                  pl.BlockSpec((hpb, LANES), lambda m, n: (n, 0))],
        out_specs=pl.BlockSpec((hpb, None, bm // dil, dil * LANES), lambda m, n: (n, m // spb, m % spb, 0)),
        out_shape=jax.ShapeDtypeStruct((N // LANES, batch, seq // dil, dil * LANES), BF16),
        scratch_shapes=[pltpu.VMEM((hpb, bm, LANES), F32)] if dil > 1 else [],
        compiler_params=_params(("parallel", "parallel"), vmem),
        name=f"qkv_proj_d{dil}",
    )(hn, w, head_gains)


def _t5_bucket_np(dist):
    dist = np.asarray(dist, np.int64)
    d_f = np.maximum(dist, 1).astype(np.float64)
    large = REL_MAX_EXACT + (np.log(d_f / REL_MAX_EXACT) / math.log(REL_MAX_DISTANCE / REL_MAX_EXACT)
                             * (N_BUCKETS - REL_MAX_EXACT)).astype(np.int64)
    large = np.minimum(large, N_BUCKETS - 1)
    return np.where(dist < REL_MAX_EXACT, dist, large)


def _band_buckets():
    out = []
    for window, dilation in A_GROUPS:
        C = window // dilation
        qi = np.arange(C)[:, None]
        kk = np.arange(2 * C)[None, :]
        steps = qi + C - kk
        band = (steps >= 0) & (steps <= C)
        bucket = _t5_bucket_np(np.maximum(steps, 0) * dilation)
        out.append(np.where(band, bucket, -1).astype(np.int32))
    return np.stack(out)


def _dilated_body(tab_ref, bkt_ref, *refs, n_heads, seq):
    n_g = len(A_GROUPS)
    qkv_refs = refs[:3 * n_g]
    o_ref, acc_scr, m_scr = refs[3 * n_g:]
    h = pl.program_id(1)

    for g, (window, dil) in enumerate(A_GROUPS):
        C = window // dil
        q_ref, k_ref, v_ref = qkv_refs[3 * g:3 * g + 3]
        n_blocks = seq // dil // C
        first_group = g == 0

        bkt = bkt_ref[g]
        bias = jnp.full((C, 2 * C), NEG, F32)
        for b in range(N_BUCKETS):
            bias = jnp.where(bkt == b, tab_ref[(g * n_heads + h) * N_BUCKETS + b], bias)

        def attend(c, r, first_block, bias=bias, q_ref=q_ref, k_ref=k_ref, v_ref=v_ref, C=C, dil=dil,
                   first_group=first_group):
            lanes = slice(r * LANES, (r + 1) * LANES)
            if first_block:
                q, kk, vv, bias_blk = q_ref[0:C, lanes], k_ref[0:C, lanes], v_ref[0:C, lanes], bias[:, C:]
            else:
                r0 = c * C if isinstance(c, int) else pl.multiple_of(c * C, C)
                rp = (c - 1) * C if isinstance(c, int) else pl.multiple_of((c - 1) * C, C)
                q, kk, vv, bias_blk = (q_ref[pl.ds(r0, C), lanes], k_ref[pl.ds(rp, 2 * C), lanes],
                                       v_ref[pl.ds(rp, 2 * C), lanes], bias)
            s = _dot_nt(q, kk) + bias_blk
            m = jnp.max(s, axis=-1, keepdims=True)
            p = jnp.exp2(s - m)
            o1 = _dot(p.astype(BF16), jnp.concatenate([vv, jnp.ones_like(vv)], axis=1))

            if dil == 1:
                pos = pl.ds(0 if first_block else r0, C)
            else:
                pos = pl.ds((0 if first_block else c * C) * dil + r, C, stride=dil)
            if first_group:
                acc_scr[0, pos, :] = o1[:, 0:LANES]
                acc_scr[1, pos, :] = o1[:, LANES:2 * LANES]
                m_scr[pos, :] = jnp.broadcast_to(m, (C, LANES))
            else:
                m_old = m_scr[pos, :]
                m_new = jnp.maximum(m_old, m)
                a = jnp.exp2(m_old - m_new)
                b = jnp.exp2(m - m_new)
                acc_scr[0, pos, :] = a * acc_scr[0, pos, :] + b * o1[:, 0:LANES]
                acc_scr[1, pos, :] = a * acc_scr[1, pos, :] + b * o1[:, LANES:2 * LANES]
                m_scr[pos, :] = m_new

        if dil == 1:
            per_iter = min(ATTN_BLOCKS_PER_ITER, n_blocks)
            assert n_blocks % per_iter == 0
            for u in range(per_iter):
                attend(u, 0, u == 0)

            def body(bi, carry, attend=attend, per_iter=per_iter):
                for u in range(per_iter):
                    attend(bi * per_iter + u, 0, False)
                return carry

            lax.fori_loop(1, n_blocks // per_iter, body, 0)
        else:
            for r in range(dil):
                for c in range(n_blocks):
                    attend(c, r, c == 0)

    rows = ATTN_BLOCKS_PER_ITER * LANES
    for i in range(seq // rows):
        sl = slice(i * rows, (i + 1) * rows)
        o_ref[sl, :] = (acc_scr[0, sl, :] / acc_scr[1, sl, :]).astype(o_ref.dtype)


def _dilated_attention(qkvs, rel_bias, batch, seq, n_heads):
    tab = (jnp.transpose(rel_bias, (1, 2, 0)) * math.log2(math.e)).reshape(-1).astype(F32)
    bkt = jnp.asarray(_band_buckets())
    C = A_GROUPS[0][0] // A_GROUPS[0][1]

    in_specs = [pl.BlockSpec(memory_space=pltpu.SMEM),
                pl.BlockSpec(bkt.shape, lambda b, h: (0, 0, 0))]
    args = [tab, bkt]
    for g, (window, dil) in enumerate(A_GROUPS):
        assert window // dil == C and seq % (dil * C) == 0
        for s in range(3):
            in_specs.append(pl.BlockSpec((None, None, seq // dil, dil * LANES),
                                         lambda b, h, s=s: (s * n_heads + h, b, 0, 0)))
            args.append(qkvs[g])
    vmem = len(A_GROUPS) * 3 * 2 * seq * LANES * 2 + 2 * seq * LANES * 2 + seq * 3 * LANES * 4
    return pl.pallas_call(
        functools.partial(_dilated_body, n_heads=n_heads, seq=seq),
        grid=(batch, n_heads),
        in_specs=in_specs,
        out_specs=pl.BlockSpec((seq, LANES), lambda b, h: (b, h)),
        out_shape=jax.ShapeDtypeStruct((batch * seq, n_heads * LANES), BF16),
        scratch_shapes=[pltpu.VMEM((2, seq, LANES), F32), pltpu.VMEM((seq, LANES), F32)],
        compiler_params=_params(("parallel", "parallel"), vmem),
        name="dilated_attention",
    )(*args)


def _mm_res_body(x_ref, w_ref, res_ref, out_ref):
    out_ref[...] = res_ref[...] + _dot(x_ref[...], w_ref[...])


def _mm_res(x, w, res, bm_pref, bn_pref):
    M, K = x.shape
    N = w.shape[1]
    bm = _pick(M, bm_pref, BF16_SUBLANES)
    bn = _pick(N, bn_pref, LANES)
    vmem = 2 * bm * K * 2 + 2 * K * bn * 2 + 5 * bm * bn * 4
    return pl.pallas_call(
        _mm_res_body,
        grid=(M // bm, N // bn),
        in_specs=[pl.BlockSpec((bm, K), lambda m, n: (m, 0)),
                  pl.BlockSpec((K, bn), lambda m, n: (0, n)),
                  pl.BlockSpec((bm, bn), lambda m, n: (m, n))],
        out_specs=pl.BlockSpec((bm, bn), lambda m, n: (m, n)),
        out_shape=jax.ShapeDtypeStruct((M, N), F32),
        compiler_params=_params(("parallel", "parallel"), vmem),
        name="mm_res",
    )(x, w, res)


def _ffn_up_body(x_ref, halo_ref, wg_ref, wu_ref, cwg_ref, cwu_ref, cbg_ref, cbu_ref, o_ref,
                 lhs_scr, accg_scr, accu_scr, *, blocks_per_seq, nb):
    i = pl.program_id(0)
    m = i // nb
    n = i % nb
    first_of_seq = (m % blocks_per_seq) == 0

    @pl.when(i == 0)
    def _():
        accg_scr[...] = jnp.zeros(accg_scr.shape, accg_scr.dtype)
        accu_scr[...] = jnp.zeros(accu_scr.shape, accu_scr.dtype)

    @pl.when(n == 0)
    def _():
        lhs_scr[HALO:, :] = x_ref[...]

    @pl.when((n == 0) & first_of_seq)
    def _():
        lhs_scr[:HALO, :] = jnp.zeros((HALO, lhs_scr.shape[1]), lhs_scr.dtype)

    @pl.when((n == 0) & jnp.logical_not(first_of_seq))
    def _():
        lhs_scr[:HALO, :] = halo_ref[...]

    def conv(acc, cw_ref, cb_ref):
        a1 = pltpu.roll(acc, 1, 0)
        a2 = pltpu.roll(acc, 2, 0)
        return (acc[HALO:] * cw_ref[2:3, :] + a1[HALO:] * cw_ref[1:2, :]
                + a2[HALO:] * cw_ref[0:1, :] + cb_ref[...])

    bm = o_ref.shape[0]
    rc = bm // ROW_CHUNKS
    for c in range(ROW_CHUNKS):
        rows = pl.ds(c * rc, rc + HALO)
        gate = conv(accg_scr[rows, :], cwg_ref, cbg_ref)
        up = conv(accu_scr[rows, :], cwu_ref, cbu_ref)
        o_ref[pl.ds(c * rc, rc), :] = (gate / (1.0 + jnp.exp(-gate)) * up).astype(o_ref.dtype)

        n_rows = rc + (HALO if c == ROW_CHUNKS - 1 else 0)
        lhs = lhs_scr[pl.ds(c * rc, n_rows), :]
        accg_scr[pl.ds(c * rc, n_rows), :] = _dot(lhs, wg_ref[...])
        accu_scr[pl.ds(c * rc, n_rows), :] = _dot(lhs, wu_ref[...])


def _ffn_up(hn, w_up, conv_w, conv_b, seq):
    T, D = hn.shape
    F = w_up.shape[1] // 2
    bm = _pick(seq, 1024, BF16_SUBLANES)
    bn = _pick(F, 256, LANES)
    nb = F // bn
    n_tiles = (T // bm) * nb
    conv_b2 = conv_b.reshape(1, 2 * F)
    vmem = (2 * bm * D * 2 + (bm + HALO) * D * 2 + 4 * D * bn * 2 + 2 * bm * bn * 2
            + 10 * (bm + HALO) * bn * 4)
    body = functools.partial(_ffn_up_body, blocks_per_seq=seq // bm, nb=nb)
    halo_blocks = bm // HALO

    def cur(i):
        return jnp.minimum(i, n_tiles - 1)

    def prev(i):
        return jnp.maximum(i - 1, 0)

    return pl.pallas_call(
        body,
        grid=(n_tiles + 1,),
        in_specs=[pl.BlockSpec((bm, D), lambda i: (cur(i) // nb, 0)),
                  pl.BlockSpec((HALO, D), lambda i: (jnp.maximum(cur(i) // nb * halo_blocks - 1, 0), 0)),
                  pl.BlockSpec((D, bn), lambda i: (0, cur(i) % nb)),
                  pl.BlockSpec((D, bn), lambda i: (0, cur(i) % nb + nb)),
                  pl.BlockSpec((CONV_WIDTH, bn), lambda i: (0, prev(i) % nb)),
                  pl.BlockSpec((CONV_WIDTH, bn), lambda i: (0, prev(i) % nb + nb)),
                  pl.BlockSpec((1, bn), lambda i: (0, prev(i) % nb)),
                  pl.BlockSpec((1, bn), lambda i: (0, prev(i) % nb + nb))],
        out_specs=pl.BlockSpec((bm, bn), lambda i: (prev(i) // nb, prev(i) % nb)),
        out_shape=jax.ShapeDtypeStruct((T, F), BF16),
        scratch_shapes=[pltpu.VMEM((bm + HALO, D), BF16),
                        pltpu.VMEM((bm + HALO, bn), F32),
                        pltpu.VMEM((bm + HALO, bn), F32)],
        compiler_params=_params(("arbitrary",), vmem),
        name="ffn_up",
    )(hn, hn, w_up, w_up, conv_w, conv_w, conv_b2, conv_b2)


def _rope_tables(seq):
    pos = jnp.arange(seq, dtype=F32)
    inv = ROPE_THETA ** (-jnp.arange(0, ROPE_DIM, 2, dtype=F32) / ROPE_DIM)
    ang = pos[:, None] * inv[None, :]
    cos, sin = jnp.cos(ang), jnp.sin(ang)
    zeros = jnp.zeros((seq, LANES - ROPE_DIM), F32)
    return (jnp.concatenate([cos, cos, zeros], axis=-1), jnp.concatenate([-sin, sin, zeros], axis=-1))


def _with_swapped_rope(w_rope):
    half = ROPE_DIM // 2
    return jnp.concatenate([w_rope, w_rope[..., half:], w_rope[..., :half]], axis=-1)


def _rope(y, c_tab, s_tab):
    return y * c_tab + pltpu.roll(y, ROPE_DIM, 1) * s_tab


def _kv_down_body(x_ref, w_ref, gl_ref, gr_ref, c_ref, s_ref, ckv_ref, kr_ref, ssq_ref, *, kv_rank):
    acc = _dot(x_ref[...], w_ref[...])
    c = acc[:, :kv_rank]
    ckv_ref[...] = (c * _rms_scale(c, kv_rank) * gl_ref[...]).astype(ckv_ref.dtype)
    kr = acc[:, kv_rank:kv_rank + LANES]
    ssq_ref[...] = jnp.broadcast_to(0.5 * jnp.sum(kr * kr, axis=-1, keepdims=True), kr.shape)
    kr_ref[...] = _rope(kr * gr_ref[...], c_ref[...], s_ref[...])


def _kv_down(hn, w_pad, latent_gain, rope_gain_pad, tables, seq, kv_rank):
    T, D = hn.shape
    N = w_pad.shape[1]
    bm = _pick(seq, 1024, BF16_SUBLANES)
    spb = seq // bm
    vmem = 2 * bm * D * 2 + 2 * D * N * 2 + 4 * bm * N * 4 + 2 * bm * kv_rank * 2 + 10 * bm * LANES * 4
    tab_spec = pl.BlockSpec((bm, LANES), lambda m: (m % spb, 0))
    return pl.pallas_call(
        functools.partial(_kv_down_body, kv_rank=kv_rank),
        grid=(T // bm,),
        in_specs=[pl.BlockSpec((bm, D), lambda m: (m, 0)),
                  pl.BlockSpec((D, N), lambda m: (0, 0)),
                  pl.BlockSpec((1, kv_rank), lambda m: (0, 0)),
                  pl.BlockSpec((1, LANES), lambda m: (0, 0)),
                  tab_spec, tab_spec],
        out_specs=[pl.BlockSpec((bm, kv_rank), lambda m: (m, 0)),
                   pl.BlockSpec((bm, LANES), lambda m: (m, 0)),
                   pl.BlockSpec((bm, LANES), lambda m: (m, 0))],
        out_shape=[jax.ShapeDtypeStruct((T, kv_rank), BF16),
                   jax.ShapeDtypeStruct((T, LANES), F32),
                   jax.ShapeDtypeStruct((T, LANES), F32)],
        compiler_params=_params(("parallel",), vmem),
        name="kv_down",
    )(hn, w_pad, latent_gain, rope_gain_pad, *tables)


def _kv_up_body(c_ref, w_ref, kr_ref, ssq_ref, gn_ref, k_ref, v_ref, *, heads_per_blk):
    acc = _dot(c_ref[...], w_ref[...])
    kr = kr_ref[...]
    ssq = ssq_ref[...]
    width = NOPE_DIM + V_DIM
    for j in range(heads_per_blk):
        kn = acc[:, j * width:j * width + NOPE_DIM]
        f = lax.rsqrt((jnp.sum(kn * kn, axis=-1, keepdims=True) + ssq) / QK_DIM + EPS)
        k_ref[j, :, 0:NOPE_DIM] = (kn * f * gn_ref[...]).astype(k_ref.dtype)
        k_ref[j, :, NOPE_DIM:2 * LANES] = (kr * f).astype(k_ref.dtype)
        v_ref[j, :, 0:V_DIM] = acc[:, j * width + NOPE_DIM:(j + 1) * width].astype(v_ref.dtype)
        v_ref[j, :, V_DIM:2 * LANES] = jnp.ones((acc.shape[0], 2 * LANES - V_DIM), v_ref.dtype)


def _kv_up(ckv, w, kr, ssq, nope_gain, n_heads):
    T, K = ckv.shape
    width = NOPE_DIM + V_DIM
    bm = _pick(T, 1024, BF16_SUBLANES)
    hpb = _pick(n_heads, 4, 1)
    bn = hpb * width
    vmem = 2 * bm * K * 2 + 2 * K * bn * 2 + 4 * bm * LANES * 4 + 2 * bm * bn * 4 + 2 * hpb * bm * 3 * LANES * 2
    return pl.pallas_call(
        functools.partial(_kv_up_body, heads_per_blk=hpb),
        grid=(T // bm, n_heads // hpb),
        in_specs=[pl.BlockSpec((bm, K), lambda m, n: (m, 0)),
                  pl.BlockSpec((K, bn), lambda m, n: (0, n)),
                  pl.BlockSpec((bm, LANES), lambda m, n: (m, 0)),
                  pl.BlockSpec((bm, LANES), lambda m, n: (m, 0)),
                  pl.BlockSpec((1, NOPE_DIM), lambda m, n: (0, 0))],
        out_specs=[pl.BlockSpec((hpb, bm, 2 * LANES), lambda m, n: (n, m, 0)),
                   pl.BlockSpec((hpb, bm, 2 * LANES), lambda m, n: (n, m, 0))],
        out_shape=[jax.ShapeDtypeStruct((n_heads, T, 2 * LANES), BF16),
                   jax.ShapeDtypeStruct((n_heads, T, 2 * LANES), BF16)],
        compiler_params=_params(("parallel", "parallel"), vmem),
        name="kv_up",
    )(ckv, w, kr, ssq, nope_gain)


def _q_down_body(x_ref, w_ref, g_ref, o_ref):
    acc = _dot(x_ref[...], w_ref[...])
    o_ref[...] = (acc * _rms_scale(acc, acc.shape[-1]) * g_ref[...]).astype(o_ref.dtype)


def _q_down(hn, w, gain):
    T, D = hn.shape
    N = w.shape[1]
    bm = _pick(T, 1024, BF16_SUBLANES)
    vmem = 2 * bm * D * 2 + 2 * D * N * 2 + 3 * bm * N * 4 + 2 * bm * N * 2
    return pl.pallas_call(
        _q_down_body,
        grid=(T // bm,),
        in_specs=[pl.BlockSpec((bm, D), lambda m: (m, 0)),
                  pl.BlockSpec((D, N), lambda m: (0, 0)),
                  pl.BlockSpec((1, N), lambda m: (0, 0))],
        out_specs=pl.BlockSpec((bm, N), lambda m: (m, 0)),
        out_shape=jax.ShapeDtypeStruct((T, N), BF16),
        compiler_params=_params(("parallel",), vmem),
        name="q_down",
    )(hn, w, gain)


def _q_up_body(c_ref, w_ref, gn_ref, gr_ref, ct_ref, st_ref, q_ref, *, heads_per_blk):
    acc = _dot(c_ref[...], w_ref[...])
    ct, st = ct_ref[...], st_ref[...]
    for j in range(heads_per_blk):
        tn = acc[:, 2 * j * LANES:(2 * j + 1) * LANES]
        tr = acc[:, (2 * j + 1) * LANES:(2 * j + 2) * LANES]
        ssq = jnp.sum(tn * tn + 0.5 * (tr * tr), axis=-1, keepdims=True)
        f = lax.rsqrt(ssq / QK_DIM + EPS)
        q_ref[j, :, 0:LANES] = (tn * f * gn_ref[...]).astype(q_ref.dtype)
        q_ref[j, :, LANES:2 * LANES] = (_rope(tr * gr_ref[...], ct, st) * f).astype(q_ref.dtype)


def _q_up(cq, w_pad, nope_gain, rope_gain_pad, tables, seq, n_heads):
    T, K = cq.shape
    bm = _pick(seq, 1024, BF16_SUBLANES)
    spb = seq // bm
    hpb = _pick(n_heads, 4, 1)
    bn = hpb * 2 * LANES
    vmem = 2 * bm * K * 2 + 2 * K * bn * 2 + 6 * bm * LANES * 4 + 3 * bm * bn * 4 + 2 * bm * bn * 2
    tab_spec = pl.BlockSpec((bm, LANES), lambda m, n: (m % spb, 0))
    return pl.pallas_call(
        functools.partial(_q_up_body, heads_per_blk=hpb),
        grid=(T // bm, n_heads // hpb),
        in_specs=[pl.BlockSpec((bm, K), lambda m, n: (m, 0)),
                  pl.BlockSpec((K, bn), lambda m, n: (0, n)),
                  pl.BlockSpec((1, LANES), lambda m, n: (0, 0)),
                  pl.BlockSpec((1, LANES), lambda m, n: (0, 0)),
                  tab_spec, tab_spec],
        out_specs=pl.BlockSpec((hpb, bm, 2 * LANES), lambda m, n: (n, m, 0)),
        out_shape=jax.ShapeDtypeStruct((n_heads, T, 2 * LANES), BF16),
        compiler_params=_params(("parallel", "parallel"), vmem),
        name="q_up",
    )(cq, w_pad, nope_gain, rope_gain_pad, *tables)


def _mla_body(q_ref, k_ref, v_ref, o_ref, *, blk, heads_per_step):
    qi = pl.program_id(2)
    qs = [q_ref[t] for t in range(heads_per_step)]

    def chunk(j, carry, diagonal):
        k0 = pl.multiple_of(j * blk, blk)
        out = []
        for t in range(heads_per_step):
            m, acc = carry[t]
            s = _dot_nt(qs[t], k_ref[t, pl.ds(k0, blk), :])
            if diagonal:
                row = lax.broadcasted_iota(jnp.int32, s.shape, 0)
                col = lax.broadcasted_iota(jnp.int32, s.shape, 1)
                s = jnp.where(col <= row, s, NEG)
            m_new = jnp.maximum(m, jnp.max(s, axis=-1, keepdims=True))
            p = jnp.exp2(s - m_new)
            acc = jnp.exp2(m - m_new) * acc + _dot(p.astype(BF16), v_ref[t, pl.ds(k0, blk), :])
            out.append((m_new, acc))
        return tuple(out)

    init = tuple((jnp.full((blk, 1), NEG, F32), jnp.zeros((blk, 2 * LANES), F32))
                 for _ in range(heads_per_step))
    carry = lax.fori_loop(0, qi, lambda j, c: chunk(j, c, False), init)
    carry = chunk(qi, carry, True)
    for t in range(heads_per_step):
        acc = carry[t][1]
        o_ref[:, t * V_DIM:(t + 1) * V_DIM] = (acc[:, :V_DIM] / acc[:, V_DIM:2 * V_DIM]).astype(o_ref.dtype)


def _mla_attention(q, k, v1, batch, seq):
    n_heads, T, _ = q.shape
    blk = _pick(seq, 512, BF16_SUBLANES)
    nq = seq // blk
    hps = _pick(n_heads, 4, 1)
    vmem = hps * (2 * blk * 2 * LANES * 2 + 2 * seq * 4 * LANES * 2 + 2 * blk * V_DIM * 2 + 6 * blk * blk * 4)
    return pl.pallas_call(
        functools.partial(_mla_body, blk=blk, heads_per_step=hps),
        grid=(batch, n_heads // hps, nq),
        in_specs=[pl.BlockSpec((hps, blk, 2 * LANES), lambda b, h, i: (h, b * nq + i, 0)),
                  pl.BlockSpec((hps, seq, 2 * LANES), lambda b, h, i: (h, b, 0)),
                  pl.BlockSpec((hps, seq, 2 * LANES), lambda b, h, i: (h, b, 0))],
        out_specs=pl.BlockSpec((blk, hps * V_DIM), lambda b, h, i: (b * nq + i, h)),
        out_shape=jax.ShapeDtypeStruct((T, n_heads * V_DIM), BF16),
        compiler_params=_params(("parallel", "parallel", "parallel"), vmem),
        name="mla_attention",
    )(q, k, v1)


def _conv_ffn(h, gain, w_up, conv_w, conv_b, w_down, seq):
    (hn,) = _rmsnorm(h, gain.reshape(1, -1))
    act = _ffn_up(hn, w_up.astype(BF16), conv_w, conv_b, seq)
    return _mm_res(act, w_down.astype(BF16), h, 512, 512)


def kernel(x, a_attn_norm, a_w_qkv, a_q_norm, a_k_norm, a_w_o, rel_bias, kv_norm, kv_w_down,
           kv_latent_norm, kv_w_up, kv_k_norm, b_attn_norm, b_w_dq, b_q_latent_norm, b_w_uq, b_q_norm,
           b_w_o, ffn_norm, ffn_w_up, ffn_conv_w, ffn_conv_b, ffn_w_down):
    batch, seq, d_model = x.shape
    T = batch * seq
    n_g = len(A_GROUPS)
    n_a_layers = a_w_qkv.shape[0]
    n_b_layers = b_w_dq.shape[0]
    a_heads = a_w_qkv.shape[2] // (3 * n_g * LANES)
    b_heads = b_w_o.shape[1] // V_DIM
    kv_rank = kv_w_down.shape[1] - ROPE_DIM
    tables = _rope_tables(seq)

    h = x.reshape(T, d_model)
    k_shared = v_shared = None
    for layer in range(n_a_layers + n_b_layers):
        if layer < n_a_layers:
            (hn,) = _rmsnorm(h, a_attn_norm[layer].reshape(1, -1))
            w_qkv = a_w_qkv[layer].reshape(d_model, 3, n_g, a_heads * LANES)
            q_scale = LANES ** -0.5 * math.log2(math.e)
            qkvs = []
            for g, (_, dil) in enumerate(A_GROUPS):
                gains = jnp.stack([a_q_norm[layer, g] * q_scale, a_k_norm[layer, g], jnp.ones((LANES,), F32)])
                qkvs.append(_qkv_proj(hn, w_qkv[:, :, g, :].reshape(d_model, -1).astype(BF16),
                                      jnp.repeat(gains, a_heads, axis=0), batch, seq, dil))
            attn = _dilated_attention(qkvs, rel_bias, batch, seq, a_heads)
            h = _mm_res(attn, a_w_o[layer].astype(BF16), h, 1024, 512)
        else:
            i = layer - n_a_layers
            if layer == n_a_layers:
                hn_kv, hn_q = _rmsnorm(h, jnp.stack([kv_norm, b_attn_norm[i]]))
                w_kvd = jnp.concatenate([kv_w_down[:, :kv_rank], _with_swapped_rope(kv_w_down[:, kv_rank:])],
                                        axis=1).astype(BF16)
                ckv, kr, ssq = _kv_down(hn_kv, w_kvd, kv_latent_norm.reshape(1, -1),
                                        _with_swapped_rope(kv_k_norm[NOPE_DIM:]).reshape(1, LANES),
                                        tables, seq, kv_rank)
                k_shared, v_shared = _kv_up(ckv, kv_w_up.astype(BF16), kr, ssq,
                                            kv_k_norm[:NOPE_DIM].reshape(1, -1), b_heads)
            else:
                (hn_q,) = _rmsnorm(h, b_attn_norm[i].reshape(1, -1))
            cq = _q_down(hn_q, b_w_dq[i].astype(BF16), b_q_latent_norm[i].reshape(1, -1))
            q_rank = b_w_uq.shape[1]
            w_uq = b_w_uq[i].reshape(q_rank, b_heads, QK_DIM)
            w_uq = jnp.concatenate([w_uq[..., :NOPE_DIM], _with_swapped_rope(w_uq[..., NOPE_DIM:])], axis=-1)
            w_uq = w_uq.reshape(q_rank, b_heads * 2 * LANES).astype(BF16)
            q_gain = b_q_norm[i] * (QK_DIM ** -0.5 * math.log2(math.e))
            q = _q_up(cq, w_uq, q_gain[:NOPE_DIM].reshape(1, -1),
                      _with_swapped_rope(q_gain[NOPE_DIM:]).reshape(1, LANES), tables, seq, b_heads)
            attn = _mla_attention(q, k_shared, v_shared, batch, seq)
            h = _mm_res(attn, b_w_o[i].astype(BF16), h, 1024, 512)
        h = _conv_ffn(h, ffn_norm[layer], ffn_w_up[layer], ffn_conv_w[layer], ffn_conv_b[layer],
                      ffn_w_down[layer], seq)
    return h.reshape(batch, seq, d_model)
```

```python
import functools
import math

import numpy as np
import jax
import jax.numpy as jnp
from jax import lax
from jax.experimental import pallas as pl
from jax.experimental.pallas import tpu as pltpu

F32 = jnp.float32
BF16 = jnp.bfloat16

EPS = 1e-6
A_GROUPS = ((128, 1), (512, 4), (2048, 16))
N_BUCKETS = 32
REL_MAX_EXACT = N_BUCKETS // 2
REL_MAX_DISTANCE = 2048
NOPE_DIM = 128
ROPE_DIM = 64
V_DIM = 128
QK_DIM = NOPE_DIM + ROPE_DIM
ROPE_THETA = 10000.0
CONV_WIDTH = 3

LANES = 128
BF16_SUBLANES = 16
V7X_VMEM_BYTES = 64 * 1024 * 1024
HALO = BF16_SUBLANES
ATTN_BLOCKS_PER_ITER = 8
ROW_CHUNKS = 4
NEG = -1e30


def _pick(dim, pref, align):
    best = None
    for cand in range(align, min(dim, pref) + 1, align):
        if dim % cand == 0:
            best = cand
    assert best is not None, (dim, pref, align)
    return best


def _params(semantics, vmem_bytes):
    limit = min(int(vmem_bytes * 1.25) + (4 << 20), V7X_VMEM_BYTES - (6 << 20))
    return pltpu.CompilerParams(dimension_semantics=semantics, vmem_limit_bytes=limit)


def _dot(a, b):
    return jnp.dot(a, b, preferred_element_type=F32)


def _dot_nt(a, b):
    return lax.dot_general(a, b, (((1,), (1,)), ((), ())), preferred_element_type=F32)


def _rms_scale(t, dim):
    return lax.rsqrt(jnp.sum(t * t, axis=-1, keepdims=True) / dim + EPS)


def _rmsnorm_body(x_ref, g_ref, *o_refs):
    x = x_ref[...]
    y = x * _rms_scale(x, x.shape[-1])
    for i, o_ref in enumerate(o_refs):
        o_ref[...] = (y * g_ref[i:i + 1, :]).astype(o_ref.dtype)


def _rmsnorm(x, gains):
    M, D = x.shape
    n = gains.shape[0]
    br = _pick(M, 256, 8)
    vmem = 2 * br * D * 4 + 2 * n * br * D * 2 + 2 * br * D * 4
    outs = pl.pallas_call(
        _rmsnorm_body,
        grid=(M // br,),
        in_specs=[pl.BlockSpec((br, D), lambda i: (i, 0)),
                  pl.BlockSpec((n, D), lambda i: (0, 0))],
        out_specs=[pl.BlockSpec((br, D), lambda i: (i, 0))] * n,
        out_shape=[jax.ShapeDtypeStruct((M, D), BF16)] * n,
        compiler_params=_params(("parallel",), vmem),
        name="rmsnorm",
    )(x, gains)
    return outs


def _qkv_body(x_ref, w_ref, g_ref, o_ref, acc_scr, *, nb, n_norm_blocks, heads_per_blk, dil):
    i = pl.program_id(0)
    bm = acc_scr.shape[1]
    rc = bm // ROW_CHUNKS

    @pl.when(i == 0)
    def _():
        acc_scr[...] = jnp.zeros(acc_scr.shape, acc_scr.dtype)

    def step(normalise):
        for c in range(ROW_CHUNKS):
            for j in range(heads_per_blk):
                for r in range(dil):
                    t = acc_scr[j, pl.ds(c * rc + r, rc // dil, stride=dil), :]
                    if normalise:
                        t = t * _rms_scale(t, LANES) * g_ref[j:j + 1, :]
                    o_ref[j, pl.ds(c * rc // dil, rc // dil), r * LANES:(r + 1) * LANES] = t.astype(o_ref.dtype)
            res = _dot(x_ref[pl.ds(c * rc, rc), :], w_ref[...])
            for j in range(heads_per_blk):
                acc_scr[j, pl.ds(c * rc, rc), :] = res[:, j * LANES:(j + 1) * LANES]

    prev_normalised = (jnp.maximum(i - 1, 0) % nb) < n_norm_blocks
    pl.when(prev_normalised)(lambda: step(True))
    pl.when(jnp.logical_not(prev_normalised))(lambda: step(False))


def _qkv_proj(hn, w_all, layer, group, head_gains, batch, seq, dil):
    T, D = hn.shape
    n_g = len(A_GROUPS)
    width = w_all.shape[2] // (3 * n_g)
    bm = _pick(seq, 1024, BF16_SUBLANES * dil * ROW_CHUNKS)
    bn = _pick(width, 1024, LANES)
    hpb = bn // LANES
    spb = seq // bm
    bps = width // bn
    nb = 3 * bps
    n_tiles = (T // bm) * nb
    vmem = 2 * bm * D * 2 + 2 * D * bn * 2 + 2 * bm * bn * 2 + 3 * bm * bn * 4
    body = functools.partial(_qkv_body, nb=nb, n_norm_blocks=2 * bps, heads_per_blk=hpb, dil=dil)

    def cur(i):
        return jnp.minimum(i, n_tiles - 1)

    def prev(i):
        return jnp.maximum(i - 1, 0)

    def w_block(n):
        return (n // bps * n_g + group) * bps + n % bps

    return pl.pallas_call(
        body,
        grid=(n_tiles + 1,),
        in_specs=[pl.BlockSpec((bm, D), lambda i: (cur(i) // nb, 0)),
                  pl.BlockSpec((None, D, bn), lambda i: (layer, 0, w_block(cur(i) % nb))),
                  pl.BlockSpec((hpb, LANES), lambda i: (prev(i) % nb, 0))],
        out_specs=pl.BlockSpec((hpb, None, bm // dil, dil * LANES),
                               lambda i: (prev(i) % nb, prev(i) // nb // spb, prev(i) // nb % spb, 0)),
        out_shape=jax.ShapeDtypeStruct((nb * hpb, batch, seq // dil, dil * LANES), BF16),
        scratch_shapes=[pltpu.VMEM((hpb, bm, LANES), F32)],
        compiler_params=_params(("arbitrary",), vmem),
        name=f"qkv_proj_d{dil}",
    )(hn, w_all, head_gains)


def _t5_bucket_np(dist):
    dist = np.asarray(dist, np.int64)
    d_f = np.maximum(dist, 1).astype(np.float64)
    large = REL_MAX_EXACT + (np.log(d_f / REL_MAX_EXACT) / math.log(REL_MAX_DISTANCE / REL_MAX_EXACT)
                             * (N_BUCKETS - REL_MAX_EXACT)).astype(np.int64)
    large = np.minimum(large, N_BUCKETS - 1)
    return np.where(dist < REL_MAX_EXACT, dist, large)


def _band_buckets():
    out = []
    for window, dilation in A_GROUPS:
        C = window // dilation
        qi = np.arange(C)[:, None]
        kk = np.arange(2 * C)[None, :]
        steps = qi + C - kk
        band = (steps >= 0) & (steps <= C)
        bucket = _t5_bucket_np(np.maximum(steps, 0) * dilation)
        out.append(np.where(band, bucket, -1).astype(np.int32))
    return np.stack(out)


def _dilated_body(tab_ref, bkt_ref, *refs, n_heads, seq):
    n_g = len(A_GROUPS)
    qkv_refs = refs[:3 * n_g]
    o_ref, acc_scr, m_scr = refs[3 * n_g:]
    h = pl.program_id(1)

    for g, (window, dil) in enumerate(A_GROUPS):
        C = window // dil
        q_ref, k_ref, v_ref = qkv_refs[3 * g:3 * g + 3]
        n_blocks = seq // dil // C
        first_group = g == 0

        bkt = bkt_ref[g]
        bias = jnp.full((C, 2 * C), NEG, F32)
        for b in range(N_BUCKETS):
            bias = jnp.where(bkt == b, tab_ref[(g * n_heads + h) * N_BUCKETS + b], bias)

        def attend(c, r, first_block, bias=bias, q_ref=q_ref, k_ref=k_ref, v_ref=v_ref, C=C, dil=dil,
                   first_group=first_group):
            lanes = slice(r * LANES, (r + 1) * LANES)
            if first_block:
                q, kk, vv, bias_blk = q_ref[0:C, lanes], k_ref[0:C, lanes], v_ref[0:C, lanes], bias[:, C:]
            else:
                r0 = c * C if isinstance(c, int) else pl.multiple_of(c * C, C)
                rp = (c - 1) * C if isinstance(c, int) else pl.multiple_of((c - 1) * C, C)
                q, kk, vv, bias_blk = (q_ref[pl.ds(r0, C), lanes], k_ref[pl.ds(rp, 2 * C), lanes],
                                       v_ref[pl.ds(rp, 2 * C), lanes], bias)
            s = _dot_nt(q, kk) + bias_blk
            m = jnp.max(s, axis=-1, keepdims=True)
            p = jnp.exp2(s - m)
            o1 = _dot(p.astype(BF16), jnp.concatenate([vv, jnp.ones_like(vv)], axis=1))

            if dil == 1:
                pos = pl.ds(0 if first_block else r0, C)
            else:
                pos = pl.ds((0 if first_block else c * C) * dil + r, C, stride=dil)
            if first_group:
                acc_scr[0, pos, :] = o1[:, 0:LANES]
                acc_scr[1, pos, :] = o1[:, LANES:2 * LANES]
                m_scr[pos, :] = jnp.broadcast_to(m, (C, LANES))
            else:
                m_old = m_scr[pos, :]
                m_new = jnp.maximum(m_old, m)
                a = jnp.exp2(m_old - m_new)
                b = jnp.exp2(m - m_new)
                acc_scr[0, pos, :] = a * acc_scr[0, pos, :] + b * o1[:, 0:LANES]
                acc_scr[1, pos, :] = a * acc_scr[1, pos, :] + b * o1[:, LANES:2 * LANES]
                m_scr[pos, :] = m_new

        if dil == 1:
            per_iter = min(ATTN_BLOCKS_PER_ITER, n_blocks)
            assert n_blocks % per_iter == 0
            for u in range(per_iter):
                attend(u, 0, u == 0)

            def body(bi, carry, attend=attend, per_iter=per_iter):
                for u in range(per_iter):
                    attend(bi * per_iter + u, 0, False)
                return carry

            lax.fori_loop(1, n_blocks // per_iter, body, 0)
        else:
            for r in range(dil):
                for c in range(n_blocks):
                    attend(c, r, c == 0)

    rows = ATTN_BLOCKS_PER_ITER * LANES
    for i in range(seq // rows):
        sl = slice(i * rows, (i + 1) * rows)
        o_ref[sl, :] = (acc_scr[0, sl, :] / acc_scr[1, sl, :]).astype(o_ref.dtype)


def _dilated_attention(qkvs, rel_bias, batch, seq, n_heads):
    tab = (jnp.transpose(rel_bias, (1, 2, 0)) * math.log2(math.e)).reshape(-1).astype(F32)
    bkt = jnp.asarray(_band_buckets())
    C = A_GROUPS[0][0] // A_GROUPS[0][1]

    in_specs = [pl.BlockSpec(memory_space=pltpu.SMEM),
                pl.BlockSpec(bkt.shape, lambda b, h: (0, 0, 0))]
    args = [tab, bkt]
    for g, (window, dil) in enumerate(A_GROUPS):
        assert window // dil == C and seq % (dil * C) == 0
        for s in range(3):
            in_specs.append(pl.BlockSpec((None, None, seq // dil, dil * LANES),
                                         lambda b, h, s=s: (s * n_heads + h, b, 0, 0)))
            args.append(qkvs[g])
    vmem = len(A_GROUPS) * 3 * 2 * seq * LANES * 2 + 2 * seq * LANES * 2 + seq * 3 * LANES * 4
    return pl.pallas_call(
        functools.partial(_dilated_body, n_heads=n_heads, seq=seq),
        grid=(batch, n_heads),
        in_specs=in_specs,
        out_specs=pl.BlockSpec((seq, LANES), lambda b, h: (b, h)),
        out_shape=jax.ShapeDtypeStruct((batch * seq, n_heads * LANES), BF16),
        scratch_shapes=[pltpu.VMEM((2, seq, LANES), F32), pltpu.VMEM((seq, LANES), F32)],
        compiler_params=_params(("parallel", "parallel"), vmem),
        name="dilated_attention",
    )(*args)


def _mm_res_body(x_ref, w_ref, res_ref, out_ref):
    out_ref[...] = res_ref[...] + _dot(x_ref[...], w_ref[...])


def _mm_res(x, w_all, layer, res, bm_pref, bn_pref):
    M, K = x.shape
    N = w_all.shape[2]
    bm = _pick(M, bm_pref, BF16_SUBLANES)
    bn = _pick(N, bn_pref, LANES)
    vmem = 2 * bm * K * 2 + 2 * K * bn * 2 + 5 * bm * bn * 4
    return pl.pallas_call(
        _mm_res_body,
        grid=(M // bm, N // bn),
        in_specs=[pl.BlockSpec((bm, K), lambda m, n: (m, 0)),
                  pl.BlockSpec((None, K, bn), lambda m, n: (layer, 0, n)),
                  pl.BlockSpec((bm, bn), lambda m, n: (m, n))],
        out_specs=pl.BlockSpec((bm, bn), lambda m, n: (m, n)),
        out_shape=jax.ShapeDtypeStruct((M, N), F32),
        compiler_params=_params(("parallel", "parallel"), vmem),
        name="mm_res",
    )(x, w_all, res)


def _ffn_up_body(x_ref, halo_ref, wg_ref, wu_ref, cwg_ref, cwu_ref, cbg_ref, cbu_ref, o_ref,
                 lhs_scr, accg_scr, accu_scr, *, blocks_per_seq, nb):
    i = pl.program_id(0)
    m = i // nb
    n = i % nb
    first_of_seq = (m % blocks_per_seq) == 0

    @pl.when(i == 0)
    def _():
        accg_scr[...] = jnp.zeros(accg_scr.shape, accg_scr.dtype)
        accu_scr[...] = jnp.zeros(accu_scr.shape, accu_scr.dtype)

    @pl.when(n == 0)
    def _():
        lhs_scr[HALO:, :] = x_ref[...]

    @pl.when((n == 0) & first_of_seq)
    def _():
        lhs_scr[:HALO, :] = jnp.zeros((HALO, lhs_scr.shape[1]), lhs_scr.dtype)

    @pl.when((n == 0) & jnp.logical_not(first_of_seq))
    def _():
        lhs_scr[:HALO, :] = halo_ref[...]

    def conv(acc, cw_ref, cb_ref):
        a1 = pltpu.roll(acc, 1, 0)
        a2 = pltpu.roll(acc, 2, 0)
        return (acc[HALO:] * cw_ref[2:3, :] + a1[HALO:] * cw_ref[1:2, :]
                + a2[HALO:] * cw_ref[0:1, :] + cb_ref[...])

    bm = o_ref.shape[0]
    rc = bm // ROW_CHUNKS
    for c in range(ROW_CHUNKS):
        rows = pl.ds(c * rc, rc + HALO)
        gate = conv(accg_scr[rows, :], cwg_ref, cbg_ref)
        up = conv(accu_scr[rows, :], cwu_ref, cbu_ref)
        o_ref[pl.ds(c * rc, rc), :] = (gate / (1.0 + jnp.exp(-gate)) * up).astype(o_ref.dtype)

        n_rows = rc + (HALO if c == ROW_CHUNKS - 1 else 0)
        lhs = lhs_scr[pl.ds(c * rc, n_rows), :]
        accg_scr[pl.ds(c * rc, n_rows), :] = _dot(lhs, wg_ref[...])
        accu_scr[pl.ds(c * rc, n_rows), :] = _dot(lhs, wu_ref[...])


def _ffn_up(hn, w_up, conv_w, conv_b, layer, seq):
    T, D = hn.shape
    F = w_up.shape[2] // 2
    bm = _pick(seq, 1024, BF16_SUBLANES)
    bn = _pick(F, 256, LANES)
    nb = F // bn
    n_tiles = (T // bm) * nb
    conv_b3 = conv_b.reshape(conv_b.shape[0], 1, 2 * F)
    vmem = (2 * bm * D * 2 + (bm + HALO) * D * 2 + 4 * D * bn * 2 + 2 * bm * bn * 2
            + 10 * (bm + HALO) * bn * 4)
    body = functools.partial(_ffn_up_body, blocks_per_seq=seq // bm, nb=nb)
    halo_blocks = bm // HALO

    def cur(i):
        return jnp.minimum(i, n_tiles - 1)

    def prev(i):
        return jnp.maximum(i - 1, 0)

    return pl.pallas_call(
        body,
        grid=(n_tiles + 1,),
        in_specs=[pl.BlockSpec((bm, D), lambda i: (cur(i) // nb, 0)),
                  pl.BlockSpec((HALO, D), lambda i: (jnp.maximum(cur(i) // nb * halo_blocks - 1, 0), 0)),
                  pl.BlockSpec((None, D, bn), lambda i: (layer, 0, cur(i) % nb)),
                  pl.BlockSpec((None, D, bn), lambda i: (layer, 0, cur(i) % nb + nb)),
                  pl.BlockSpec((None, CONV_WIDTH, bn), lambda i: (layer, 0, prev(i) % nb)),
                  pl.BlockSpec((None, CONV_WIDTH, bn), lambda i: (layer, 0, prev(i) % nb + nb)),
                  pl.BlockSpec((None, 1, bn), lambda i: (layer, 0, prev(i) % nb)),
                  pl.BlockSpec((None, 1, bn), lambda i: (layer, 0, prev(i) % nb + nb))],
        out_specs=pl.BlockSpec((bm, bn), lambda i: (prev(i) // nb, prev(i) % nb)),
        out_shape=jax.ShapeDtypeStruct((T, F), BF16),
        scratch_shapes=[pltpu.VMEM((bm + HALO, D), BF16),
                        pltpu.VMEM((bm + HALO, bn), F32),
                        pltpu.VMEM((bm + HALO, bn), F32)],
        compiler_params=_params(("arbitrary",), vmem),
        name="ffn_up",
    )(hn, hn, w_up, w_up, conv_w, conv_w, conv_b3, conv_b3)


def _rope_tables(seq):
    pos = jnp.arange(seq, dtype=F32)
    inv = ROPE_THETA ** (-jnp.arange(0, ROPE_DIM, 2, dtype=F32) / ROPE_DIM)
    ang = pos[:, None] * inv[None, :]
    cos, sin = jnp.cos(ang), jnp.sin(ang)
    zeros = jnp.zeros((seq, LANES - ROPE_DIM), F32)
    return (jnp.concatenate([cos, cos, zeros], axis=-1), jnp.concatenate([-sin, sin, zeros], axis=-1))


def _with_swapped_rope(w_rope):
    half = ROPE_DIM // 2
    return jnp.concatenate([w_rope, w_rope[..., half:], w_rope[..., :half]], axis=-1)


def _rope(y, c_tab, s_tab):
    return y * c_tab + pltpu.roll(y, ROPE_DIM, 1) * s_tab


def _kv_down_body(x_ref, w_ref, gl_ref, gr_ref, c_ref, s_ref, ckv_ref, kr_ref, ssq_ref, *, kv_rank):
    acc = _dot(x_ref[...], w_ref[...])
    c = acc[:, :kv_rank]
    ckv_ref[...] = (c * _rms_scale(c, kv_rank) * gl_ref[...]).astype(ckv_ref.dtype)
    kr = acc[:, kv_rank:kv_rank + LANES]
    ssq_ref[...] = jnp.broadcast_to(0.5 * jnp.sum(kr * kr, axis=-1, keepdims=True), kr.shape)
    kr_ref[...] = _rope(kr * gr_ref[...], c_ref[...], s_ref[...])


def _kv_down(hn, w_pad, latent_gain, rope_gain_pad, tables, seq, kv_rank):
    T, D = hn.shape
    N = w_pad.shape[1]
    bm = _pick(seq, 1024, BF16_SUBLANES)
    spb = seq // bm
    vmem = 2 * bm * D * 2 + 2 * D * N * 2 + 4 * bm * N * 4 + 2 * bm * kv_rank * 2 + 10 * bm * LANES * 4
    tab_spec = pl.BlockSpec((bm, LANES), lambda m: (m % spb, 0))
    return pl.pallas_call(
        functools.partial(_kv_down_body, kv_rank=kv_rank),
        grid=(T // bm,),
        in_specs=[pl.BlockSpec((bm, D), lambda m: (m, 0)),
                  pl.BlockSpec((D, N), lambda m: (0, 0)),
                  pl.BlockSpec((1, kv_rank), lambda m: (0, 0)),
                  pl.BlockSpec((1, LANES), lambda m: (0, 0)),
                  tab_spec, tab_spec],
        out_specs=[pl.BlockSpec((bm, kv_rank), lambda m: (m, 0)),
                   pl.BlockSpec((bm, LANES), lambda m: (m, 0)),
                   pl.BlockSpec((bm, LANES), lambda m: (m, 0))],
        out_shape=[jax.ShapeDtypeStruct((T, kv_rank), BF16),
                   jax.ShapeDtypeStruct((T, LANES), F32),
                   jax.ShapeDtypeStruct((T, LANES), F32)],
        compiler_params=_params(("parallel",), vmem),
        name="kv_down",
    )(hn, w_pad, latent_gain, rope_gain_pad, *tables)


def _kv_up_body(c_ref, w_ref, kr_ref, ssq_ref, gn_ref, k_ref, v_ref, *, heads_per_blk):
    acc = _dot(c_ref[...], w_ref[...])
    kr = kr_ref[...]
    ssq = ssq_ref[...]
    width = NOPE_DIM + V_DIM
    for j in range(heads_per_blk):
        kn = acc[:, j * width:j * width + NOPE_DIM]
        f = lax.rsqrt((jnp.sum(kn * kn, axis=-1, keepdims=True) + ssq) / QK_DIM + EPS)
        k_ref[j, :, 0:NOPE_DIM] = (kn * f * gn_ref[...]).astype(k_ref.dtype)
        k_ref[j, :, NOPE_DIM:2 * LANES] = (kr * f).astype(k_ref.dtype)
        v_ref[j, :, 0:V_DIM] = acc[:, j * width + NOPE_DIM:(j + 1) * width].astype(v_ref.dtype)
        v_ref[j, :, V_DIM:2 * LANES] = jnp.ones((acc.shape[0], 2 * LANES - V_DIM), v_ref.dtype)


def _kv_up(ckv, w, kr, ssq, nope_gain, n_heads):
    T, K = ckv.shape
    width = NOPE_DIM + V_DIM
    bm = _pick(T, 1024, BF16_SUBLANES)
    hpb = _pick(n_heads, 4, 1)
    bn = hpb * width
    vmem = 2 * bm * K * 2 + 2 * K * bn * 2 + 4 * bm * LANES * 4 + 2 * bm * bn * 4 + 2 * hpb * bm * 3 * LANES * 2
    return pl.pallas_call(
        functools.partial(_kv_up_body, heads_per_blk=hpb),
        grid=(T // bm, n_heads // hpb),
        in_specs=[pl.BlockSpec((bm, K), lambda m, n: (m, 0)),
                  pl.BlockSpec((K, bn), lambda m, n: (0, n)),
                  pl.BlockSpec((bm, LANES), lambda m, n: (m, 0)),
                  pl.BlockSpec((bm, LANES), lambda m, n: (m, 0)),
                  pl.BlockSpec((1, NOPE_DIM), lambda m, n: (0, 0))],
        out_specs=[pl.BlockSpec((hpb, bm, 2 * LANES), lambda m, n: (n, m, 0)),
                   pl.BlockSpec((hpb, bm, 2 * LANES), lambda m, n: (n, m, 0))],
        out_shape=[jax.ShapeDtypeStruct((n_heads, T, 2 * LANES), BF16),
                   jax.ShapeDtypeStruct((n_heads, T, 2 * LANES), BF16)],
        compiler_params=_params(("parallel", "parallel"), vmem),
        name="kv_up",
    )(ckv, w, kr, ssq, nope_gain)


def _q_down_body(x_ref, w_ref, g_ref, o_ref):
    acc = _dot(x_ref[...], w_ref[...])
    o_ref[...] = (acc * _rms_scale(acc, acc.shape[-1]) * g_ref[...]).astype(o_ref.dtype)


def _q_down(hn, w_all, layer, gain):
    T, D = hn.shape
    N = w_all.shape[2]
    bm = _pick(T, 1024, BF16_SUBLANES)
    vmem = 2 * bm * D * 2 + 2 * D * N * 2 + 3 * bm * N * 4 + 2 * bm * N * 2
    return pl.pallas_call(
        _q_down_body,
        grid=(T // bm,),
        in_specs=[pl.BlockSpec((bm, D), lambda m: (m, 0)),
                  pl.BlockSpec((None, D, N), lambda m: (layer, 0, 0)),
                  pl.BlockSpec((1, N), lambda m: (0, 0))],
        out_specs=pl.BlockSpec((bm, N), lambda m: (m, 0)),
        out_shape=jax.ShapeDtypeStruct((T, N), BF16),
        compiler_params=_params(("parallel",), vmem),
        name="q_down",
    )(hn, w_all, gain)


def _q_up_body(c_ref, w_ref, gn_ref, gr_ref, ct_ref, st_ref, q_ref, acc_scr, *, heads_per_blk):
    @pl.when(pl.program_id(0) == 0)
    def _():
        acc_scr[...] = jnp.zeros(acc_scr.shape, acc_scr.dtype)

    rc = acc_scr.shape[0] // ROW_CHUNKS
    for c in range(ROW_CHUNKS):
        rows = pl.ds(c * rc, rc)
        ct, st = ct_ref[rows, :], st_ref[rows, :]
        for j in range(heads_per_blk):
            tn = acc_scr[rows, 2 * j * LANES:(2 * j + 1) * LANES]
            tr = acc_scr[rows, (2 * j + 1) * LANES:(2 * j + 2) * LANES]
            ssq = jnp.sum(tn * tn + 0.5 * (tr * tr), axis=-1, keepdims=True)
            f = lax.rsqrt(ssq / QK_DIM + EPS)
            q_ref[j, rows, 0:LANES] = (tn * f * gn_ref[...]).astype(q_ref.dtype)
            q_ref[j, rows, LANES:2 * LANES] = (_rope(tr * gr_ref[...], ct, st) * f).astype(q_ref.dtype)
        acc_scr[rows, :] = _dot(c_ref[rows, :], w_ref[...])


def _q_up(cq, w_pad, nope_gain, rope_gain_pad, tables, seq, n_heads):
    T, K = cq.shape
    bm = _pick(seq, 1024, BF16_SUBLANES * ROW_CHUNKS)
    spb = seq // bm
    hpb = _pick(n_heads, 4, 1)
    bn = hpb * 2 * LANES
    nb = n_heads // hpb
    n_tiles = (T // bm) * nb
    vmem = 2 * bm * K * 2 + 2 * K * bn * 2 + 6 * bm * LANES * 4 + 3 * bm * bn * 4 + 2 * bm * bn * 2

    def cur(i):
        return jnp.minimum(i, n_tiles - 1)

    def prev(i):
        return jnp.maximum(i - 1, 0)

    tab_spec = pl.BlockSpec((bm, LANES), lambda i: (prev(i) // nb % spb, 0))
    return pl.pallas_call(
        functools.partial(_q_up_body, heads_per_blk=hpb),
        grid=(n_tiles + 1,),
        in_specs=[pl.BlockSpec((bm, K), lambda i: (cur(i) // nb, 0)),
                  pl.BlockSpec((K, bn), lambda i: (0, cur(i) % nb)),
                  pl.BlockSpec((1, LANES), lambda i: (0, 0)),
                  pl.BlockSpec((1, LANES), lambda i: (0, 0)),
                  tab_spec, tab_spec],
        out_specs=pl.BlockSpec((hpb, bm, 2 * LANES), lambda i: (prev(i) % nb, prev(i) // nb, 0)),
        out_shape=jax.ShapeDtypeStruct((n_heads, T, 2 * LANES), BF16),
        scratch_shapes=[pltpu.VMEM((bm, bn), F32)],
        compiler_params=_params(("arbitrary",), vmem),
        name="q_up",
    )(cq, w_pad, nope_gain, rope_gain_pad, *tables)


def _mla_body(q_ref, k_ref, v_ref, o_ref, *, blk, heads_per_step):
    qi = pl.program_id(2)
    qs = [q_ref[t] for t in range(heads_per_step)]

    def chunk(j, carry, diagonal):
        k0 = pl.multiple_of(j * blk, blk)
        out = []
        for t in range(heads_per_step):
            m, acc = carry[t]
            s = _dot_nt(qs[t], k_ref[t, pl.ds(k0, blk), :])
            if diagonal:
                row = lax.broadcasted_iota(jnp.int32, s.shape, 0)
                col = lax.broadcasted_iota(jnp.int32, s.shape, 1)
                s = jnp.where(col <= row, s, NEG)
            m_new = jnp.maximum(m, jnp.max(s, axis=-1, keepdims=True))
            p = jnp.exp2(s - m_new)
            acc = jnp.exp2(m - m_new) * acc + _dot(p.astype(BF16), v_ref[t, pl.ds(k0, blk), :])
            out.append((m_new, acc))
        return tuple(out)

    init = tuple((jnp.full((blk, 1), NEG, F32), jnp.zeros((blk, 2 * LANES), F32))
                 for _ in range(heads_per_step))
    carry = lax.fori_loop(0, qi, lambda j, c: chunk(j, c, False), init)
    carry = chunk(qi, carry, True)
    for t in range(heads_per_step):
        acc = carry[t][1]
        o_ref[:, t * V_DIM:(t + 1) * V_DIM] = (acc[:, :V_DIM] / acc[:, V_DIM:2 * V_DIM]).astype(o_ref.dtype)


def _mla_attention(q, k, v1, batch, seq):
    n_heads, T, _ = q.shape
    blk = _pick(seq, 512, BF16_SUBLANES)
    nq = seq // blk
    hps = _pick(n_heads, 4, 1)
    vmem = hps * (2 * blk * 2 * LANES * 2 + 2 * seq * 4 * LANES * 2 + 2 * blk * V_DIM * 2 + 6 * blk * blk * 4)
    return pl.pallas_call(
        functools.partial(_mla_body, blk=blk, heads_per_step=hps),
        grid=(batch, n_heads // hps, nq),
        in_specs=[pl.BlockSpec((hps, blk, 2 * LANES), lambda b, h, i: (h, b * nq + i, 0)),
                  pl.BlockSpec((hps, seq, 2 * LANES), lambda b, h, i: (h, b, 0)),
                  pl.BlockSpec((hps, seq, 2 * LANES), lambda b, h, i: (h, b, 0))],
        out_specs=pl.BlockSpec((blk, hps * V_DIM), lambda b, h, i: (b * nq + i, h)),
        out_shape=jax.ShapeDtypeStruct((T, n_heads * V_DIM), BF16),
        compiler_params=_params(("parallel", "parallel", "parallel"), vmem),
        name="mla_attention",
    )(q, k, v1)


def _conv_ffn(h, gain, w_up, conv_w, conv_b, w_down, layer, seq):
    (hn,) = _rmsnorm(h, gain.reshape(1, -1))
    act = _ffn_up(hn, w_up, conv_w, conv_b, layer, seq)
    return _mm_res(act, w_down, layer, h, 512, 512)


def kernel(x, a_attn_norm, a_w_qkv, a_q_norm, a_k_norm, a_w_o, rel_bias, kv_norm, kv_w_down,
           kv_latent_norm, kv_w_up, kv_k_norm, b_attn_norm, b_w_dq, b_q_latent_norm, b_w_uq, b_q_norm,
           b_w_o, ffn_norm, ffn_w_up, ffn_conv_w, ffn_conv_b, ffn_w_down):
    batch, seq, d_model = x.shape
    T = batch * seq
    n_g = len(A_GROUPS)
    n_a_layers = a_w_qkv.shape[0]
    n_b_layers = b_w_dq.shape[0]
    a_heads = a_w_qkv.shape[2] // (3 * n_g * LANES)
    b_heads = b_w_o.shape[1] // V_DIM
    kv_rank = kv_w_down.shape[1] - ROPE_DIM
    tables = _rope_tables(seq)

    a_w_qkv_bf, a_w_o_bf, b_w_dq_bf, b_w_o_bf = (w.astype(BF16) for w in (a_w_qkv, a_w_o, b_w_dq, b_w_o))
    ffn_w_up_bf, ffn_w_down_bf = ffn_w_up.astype(BF16), ffn_w_down.astype(BF16)

    h = x.reshape(T, d_model)
    k_shared = v_shared = None
    for layer in range(n_a_layers + n_b_layers):
        if layer < n_a_layers:
            (hn,) = _rmsnorm(h, a_attn_norm[layer].reshape(1, -1))
            q_scale = LANES ** -0.5 * math.log2(math.e)
            qkvs = []
            for g, (_, dil) in enumerate(A_GROUPS):
                gains = jnp.stack([a_q_norm[layer, g] * q_scale, a_k_norm[layer, g], jnp.ones((LANES,), F32)])
                qkvs.append(_qkv_proj(hn, a_w_qkv_bf, layer, g, jnp.repeat(gains, a_heads, axis=0),
                                      batch, seq, dil))
            attn = _dilated_attention(qkvs, rel_bias, batch, seq, a_heads)
            h = _mm_res(attn, a_w_o_bf, layer, h, 1024, 512)
        else:
            i = layer - n_a_layers
            if layer == n_a_layers:
                hn_kv, hn_q = _rmsnorm(h, jnp.stack([kv_norm, b_attn_norm[i]]))
                w_kvd = jnp.concatenate([kv_w_down[:, :kv_rank], _with_swapped_rope(kv_w_down[:, kv_rank:])],
                                        axis=1).astype(BF16)
                ckv, kr, ssq = _kv_down(hn_kv, w_kvd, kv_latent_norm.reshape(1, -1),
                                        _with_swapped_rope(kv_k_norm[NOPE_DIM:]).reshape(1, LANES),
                                        tables, seq, kv_rank)
                k_shared, v_shared = _kv_up(ckv, kv_w_up.astype(BF16), kr, ssq,
                                            kv_k_norm[:NOPE_DIM].reshape(1, -1), b_heads)
            else:
                (hn_q,) = _rmsnorm(h, b_attn_norm[i].reshape(1, -1))
            cq = _q_down(hn_q, b_w_dq_bf, i, b_q_latent_norm[i].reshape(1, -1))
            q_rank = b_w_uq.shape[1]
            w_uq = b_w_uq[i].reshape(q_rank, b_heads, QK_DIM)
            w_uq = jnp.concatenate([w_uq[..., :NOPE_DIM], _with_swapped_rope(w_uq[..., NOPE_DIM:])], axis=-1)
            w_uq = w_uq.reshape(q_rank, b_heads * 2 * LANES).astype(BF16)
            q_gain = b_q_norm[i] * (QK_DIM ** -0.5 * math.log2(math.e))
            q = _q_up(cq, w_uq, q_gain[:NOPE_DIM].reshape(1, -1),
                      _with_swapped_rope(q_gain[NOPE_DIM:]).reshape(1, LANES), tables, seq, b_heads)
            attn = _mla_attention(q, k_shared, v_shared, batch, seq)
            h = _mm_res(attn, b_w_o_bf, i, h, 1024, 512)
        h = _conv_ffn(h, ffn_norm[layer], ffn_w_up_bf, ffn_conv_w, ffn_conv_b, ffn_w_down_bf, layer, seq)
    return h.reshape(batch, seq, d_model)
```

```python
import functools
import math

import numpy as np
import jax
import jax.numpy as jnp
from jax import lax
from jax.experimental import pallas as pl
from jax.experimental.pallas import tpu as pltpu

F32 = jnp.float32
BF16 = jnp.bfloat16

EPS = 1e-6
A_GROUPS = ((128, 1), (512, 4), (2048, 16))
N_BUCKETS = 32
REL_MAX_EXACT = N_BUCKETS // 2
REL_MAX_DISTANCE = 2048
NOPE_DIM = 128
ROPE_DIM = 64
V_DIM = 128
QK_DIM = NOPE_DIM + ROPE_DIM
ROPE_THETA = 10000.0
CONV_WIDTH = 3

LANES = 128
BF16_SUBLANES = 16
V7X_VMEM_BYTES = 64 * 1024 * 1024
HALO = BF16_SUBLANES
ATTN_BLOCKS_PER_ITER = 8
ROW_CHUNKS = 4
FFN_ROW_CHUNKS = 4
NEG = -1e30


def _pick(dim, pref, align):
    best = None
    for cand in range(align, min(dim, pref) + 1, align):
        if dim % cand == 0:
            best = cand
    assert best is not None, (dim, pref, align)
    return best


def _params(semantics, vmem_bytes):
    limit = min(int(vmem_bytes * 1.25) + (4 << 20), V7X_VMEM_BYTES - (6 << 20))
    return pltpu.CompilerParams(dimension_semantics=semantics, vmem_limit_bytes=limit)


def _dot(a, b):
    return jnp.dot(a, b, preferred_element_type=F32)


def _dot_nt(a, b):
    return lax.dot_general(a, b, (((1,), (1,)), ((), ())), preferred_element_type=F32)


def _rms_scale(t, dim):
    return lax.rsqrt(jnp.sum(t * t, axis=-1, keepdims=True) / dim + EPS)


def _rmsnorm_body(x_ref, g_ref, *o_refs):
    x = x_ref[...]
    y = x * _rms_scale(x, x.shape[-1])
    for i, o_ref in enumerate(o_refs):
        o_ref[...] = (y * g_ref[i:i + 1, :]).astype(o_ref.dtype)


def _rmsnorm(x, gains):
    M, D = x.shape
    n = gains.shape[0]
    br = _pick(M, 256, 8)
    vmem = 2 * br * D * 4 + 2 * n * br * D * 2 + 2 * br * D * 4
    outs = pl.pallas_call(
        _rmsnorm_body,
        grid=(M // br,),
        in_specs=[pl.BlockSpec((br, D), lambda i: (i, 0)),
                  pl.BlockSpec((n, D), lambda i: (0, 0))],
        out_specs=[pl.BlockSpec((br, D), lambda i: (i, 0))] * n,
        out_shape=[jax.ShapeDtypeStruct((M, D), BF16)] * n,
        compiler_params=_params(("parallel",), vmem),
        name="rmsnorm",
    )(x, gains)
    return outs


def _qkv_body(x_ref, w_ref, g_ref, o_ref, acc_scr, *, nb, n_norm_blocks, heads_per_blk, dil):
    i = pl.program_id(0)
    bm = acc_scr.shape[1]
    rc = bm // ROW_CHUNKS

    @pl.when(i == 0)
    def _():
        acc_scr[...] = jnp.zeros(acc_scr.shape, acc_scr.dtype)

    def step(normalise):
        for c in range(ROW_CHUNKS):
            for j in range(heads_per_blk):
                for r in range(dil):
                    t = acc_scr[j, pl.ds(c * rc + r, rc // dil, stride=dil), :]
                    if normalise:
                        t = t * _rms_scale(t, LANES) * g_ref[j:j + 1, :]
                    o_ref[j, pl.ds(c * rc // dil, rc // dil), r * LANES:(r + 1) * LANES] = t.astype(o_ref.dtype)
            res = _dot(x_ref[pl.ds(c * rc, rc), :], w_ref[...])
            for j in range(heads_per_blk):
                acc_scr[j, pl.ds(c * rc, rc), :] = res[:, j * LANES:(j + 1) * LANES]

    prev_normalised = (jnp.maximum(i - 1, 0) % nb) < n_norm_blocks
    pl.when(prev_normalised)(lambda: step(True))
    pl.when(jnp.logical_not(prev_normalised))(lambda: step(False))


def _qkv_proj(hn, w_all, layer, group, head_gains, batch, seq, dil):
    T, D = hn.shape
    n_g = len(A_GROUPS)
    width = w_all.shape[2] // (3 * n_g)
    bm = _pick(seq, 1024, BF16_SUBLANES * dil * ROW_CHUNKS)
    bn = _pick(width, 1024, LANES)
    hpb = bn // LANES
    spb = seq // bm
    bps = width // bn
    nb = 3 * bps
    n_tiles = (T // bm) * nb
    vmem = 2 * bm * D * 2 + 2 * D * bn * 2 + 2 * bm * bn * 2 + 3 * bm * bn * 4
    body = functools.partial(_qkv_body, nb=nb, n_norm_blocks=2 * bps, heads_per_blk=hpb, dil=dil)

    def cur(i):
        return jnp.minimum(i, n_tiles - 1)

    def prev(i):
        return jnp.maximum(i - 1, 0)

    def w_block(n):
        return (n // bps * n_g + group) * bps + n % bps

    return pl.pallas_call(
        body,
        grid=(n_tiles + 1,),
        in_specs=[pl.BlockSpec((bm, D), lambda i: (cur(i) // nb, 0)),
                  pl.BlockSpec((None, D, bn), lambda i: (layer, 0, w_block(cur(i) % nb))),
                  pl.BlockSpec((hpb, LANES), lambda i: (prev(i) % nb, 0))],
        out_specs=pl.BlockSpec((hpb, None, bm // dil, dil * LANES),
                               lambda i: (prev(i) % nb, prev(i) // nb // spb, prev(i) // nb % spb, 0)),
        out_shape=jax.ShapeDtypeStruct((nb * hpb, batch, seq // dil, dil * LANES), BF16),
        scratch_shapes=[pltpu.VMEM((hpb, bm, LANES), F32)],
        compiler_params=_params(("arbitrary",), vmem),
        name=f"qkv_proj_d{dil}",
    )(hn, w_all, head_gains)


def _t5_bucket_np(dist):
    dist = np.asarray(dist, np.int64)
    d_f = np.maximum(dist, 1).astype(np.float64)
    large = REL_MAX_EXACT + (np.log(d_f / REL_MAX_EXACT) / math.log(REL_MAX_DISTANCE / REL_MAX_EXACT)
                             * (N_BUCKETS - REL_MAX_EXACT)).astype(np.int64)
    large = np.minimum(large, N_BUCKETS - 1)
    return np.where(dist < REL_MAX_EXACT, dist, large)


def _band_buckets():
    out = []
    for window, dilation in A_GROUPS:
        C = window // dilation
        qi = np.arange(C)[:, None]
        kk = np.arange(2 * C)[None, :]
        steps = qi + C - kk
        band = (steps >= 0) & (steps <= C)
        bucket = _t5_bucket_np(np.maximum(steps, 0) * dilation)
        out.append(np.where(band, bucket, -1).astype(np.int32))
    return np.stack(out)


def _dilated_body(tab_ref, bkt_ref, *refs, n_heads, seq):
    n_g = len(A_GROUPS)
    qkv_refs = refs[:3 * n_g]
    o_ref, acc_scr, m_scr = refs[3 * n_g:]
    h = pl.program_id(1)

    for g, (window, dil) in enumerate(A_GROUPS):
        C = window // dil
        q_ref, k_ref, v_ref = qkv_refs[3 * g:3 * g + 3]
        n_blocks = seq // dil // C
        first_group = g == 0

        bkt = bkt_ref[g]
        bias = jnp.full((C, 2 * C), NEG, F32)
        for b in range(N_BUCKETS):
            bias = jnp.where(bkt == b, tab_ref[(g * n_heads + h) * N_BUCKETS + b], bias)

        def attend(c, r, first_block, bias=bias, q_ref=q_ref, k_ref=k_ref, v_ref=v_ref, C=C, dil=dil,
                   first_group=first_group):
            lanes = slice(r * LANES, (r + 1) * LANES)
            if first_block:
                q, kk, vv, bias_blk = q_ref[0:C, lanes], k_ref[0:C, lanes], v_ref[0:C, lanes], bias[:, C:]
            else:
                r0 = c * C if isinstance(c, int) else pl.multiple_of(c * C, C)
                rp = (c - 1) * C if isinstance(c, int) else pl.multiple_of((c - 1) * C, C)
                q, kk, vv, bias_blk = (q_ref[pl.ds(r0, C), lanes], k_ref[pl.ds(rp, 2 * C), lanes],
                                       v_ref[pl.ds(rp, 2 * C), lanes], bias)
            s = _dot_nt(q, kk) + bias_blk
            m = jnp.max(s, axis=-1, keepdims=True)
            p = jnp.exp2(s - m)
            o1 = _dot(p.astype(BF16), jnp.concatenate([vv, jnp.ones_like(vv)], axis=1))

            if dil == 1:
                pos = pl.ds(0 if first_block else r0, C)
            else:
                pos = pl.ds((0 if first_block else c * C) * dil + r, C, stride=dil)
            if first_group:
                acc_scr[0, pos, :] = o1[:, 0:LANES]
                acc_scr[1, pos, :] = o1[:, LANES:2 * LANES]
                m_scr[pos, :] = jnp.broadcast_to(m, (C, LANES))
            else:
                m_old = m_scr[pos, :]
                m_new = jnp.maximum(m_old, m)
                a = jnp.exp2(m_old - m_new)
                b = jnp.exp2(m - m_new)
                acc_scr[0, pos, :] = a * acc_scr[0, pos, :] + b * o1[:, 0:LANES]
                acc_scr[1, pos, :] = a * acc_scr[1, pos, :] + b * o1[:, LANES:2 * LANES]
                m_scr[pos, :] = m_new

        if dil == 1:
            per_iter = min(ATTN_BLOCKS_PER_ITER, n_blocks)
            assert n_blocks % per_iter == 0
            for u in range(per_iter):
                attend(u, 0, u == 0)

            def body(bi, carry, attend=attend, per_iter=per_iter):
                for u in range(per_iter):
                    attend(bi * per_iter + u, 0, False)
                return carry

            lax.fori_loop(1, n_blocks // per_iter, body, 0)
        else:
            for r in range(dil):
                for c in range(n_blocks):
                    attend(c, r, c == 0)

    rows = ATTN_BLOCKS_PER_ITER * LANES
    for i in range(seq // rows):
        sl = slice(i * rows, (i + 1) * rows)
        o_ref[sl, :] = (acc_scr[0, sl, :] / acc_scr[1, sl, :]).astype(o_ref.dtype)


def _dilated_attention(qkvs, rel_bias, batch, seq, n_heads):
    tab = (jnp.transpose(rel_bias, (1, 2, 0)) * math.log2(math.e)).reshape(-1).astype(F32)
    bkt = jnp.asarray(_band_buckets())
    C = A_GROUPS[0][0] // A_GROUPS[0][1]

    in_specs = [pl.BlockSpec(memory_space=pltpu.SMEM),
                pl.BlockSpec(bkt.shape, lambda b, h: (0, 0, 0))]
    args = [tab, bkt]
    for g, (window, dil) in enumerate(A_GROUPS):
        assert window // dil == C and seq % (dil * C) == 0
        for s in range(3):
            in_specs.append(pl.BlockSpec((None, None, seq // dil, dil * LANES),
                                         lambda b, h, s=s: (s * n_heads + h, b, 0, 0)))
            args.append(qkvs[g])
    vmem = len(A_GROUPS) * 3 * 2 * seq * LANES * 2 + 2 * seq * LANES * 2 + seq * 3 * LANES * 4
    return pl.pallas_call(
        functools.partial(_dilated_body, n_heads=n_heads, seq=seq),
        grid=(batch, n_heads),
        in_specs=in_specs,
        out_specs=pl.BlockSpec((seq, LANES), lambda b, h: (b, h)),
        out_shape=jax.ShapeDtypeStruct((batch * seq, n_heads * LANES), BF16),
        scratch_shapes=[pltpu.VMEM((2, seq, LANES), F32), pltpu.VMEM((seq, LANES), F32)],
        compiler_params=_params(("parallel", "parallel"), vmem),
        name="dilated_attention",
    )(*args)


def _mm_res_body(x_ref, w_ref, res_ref, out_ref):
    out_ref[...] = res_ref[...] + _dot(x_ref[...], w_ref[...])


def _mm_res(x, w_all, layer, res, bm_pref, bn_pref):
    M, K = x.shape
    N = w_all.shape[2]
    bm = _pick(M, bm_pref, BF16_SUBLANES)
    bn = _pick(N, bn_pref, LANES)
    vmem = 2 * bm * K * 2 + 2 * K * bn * 2 + 5 * bm * bn * 4
    return pl.pallas_call(
        _mm_res_body,
        grid=(M // bm, N // bn),
        in_specs=[pl.BlockSpec((bm, K), lambda m, n: (m, 0)),
                  pl.BlockSpec((None, K, bn), lambda m, n: (layer, 0, n)),
                  pl.BlockSpec((bm, bn), lambda m, n: (m, n))],
        out_specs=pl.BlockSpec((bm, bn), lambda m, n: (m, n)),
        out_shape=jax.ShapeDtypeStruct((M, N), F32),
        compiler_params=_params(("parallel", "parallel"), vmem),
        name="mm_res",
    )(x, w_all, res)


def _ffn_up_body(x_hbm, wg_ref, wu_ref, cwg_ref, cwu_ref, cbg_ref, cbu_ref, o_ref,
                 lhs_scr, wg_scr, wu_scr, accg_scr, accu_scr, lhs_sem, *, blocks_per_seq, nb, n_tiles):
    i = pl.program_id(0)
    m = i // nb
    n = i % nb
    bm = o_ref.shape[0]
    first_of_seq = (m % blocks_per_seq) == 0
    new_rows = (n == 0) & (i < n_tiles)

    @pl.when(i == 0)
    def _():
        accg_scr[...] = jnp.zeros(accg_scr.shape, accg_scr.dtype)
        accu_scr[...] = jnp.zeros(accu_scr.shape, accu_scr.dtype)

    def lhs_copy(src_start, n_rows, dst_start):
        return pltpu.make_async_copy(x_hbm.at[pl.ds(src_start, n_rows), :],
                                     lhs_scr.at[pl.ds(dst_start, n_rows), :], lhs_sem)

    @pl.when(new_rows & first_of_seq)
    def _():
        copy = lhs_copy(pl.multiple_of(m * bm, bm), bm, HALO)
        copy.start()
        lhs_scr[:HALO, :] = jnp.zeros((HALO, lhs_scr.shape[1]), lhs_scr.dtype)
        copy.wait()

    @pl.when(new_rows & jnp.logical_not(first_of_seq))
    def _():
        copy = lhs_copy(pl.multiple_of(m * bm - HALO, HALO), bm + HALO, 0)
        copy.start()
        copy.wait()

    wg_scr[...] = wg_ref[...].astype(wg_scr.dtype)
    wu_scr[...] = wu_ref[...].astype(wu_scr.dtype)

    def conv(acc, cw_ref, cb_ref):
        a1 = pltpu.roll(acc, 1, 0)
        a2 = pltpu.roll(acc, 2, 0)
        return (acc[HALO:] * cw_ref[2:3, :] + a1[HALO:] * cw_ref[1:2, :]
                + a2[HALO:] * cw_ref[0:1, :] + cb_ref[...])

    rc = bm // FFN_ROW_CHUNKS
    for c in range(FFN_ROW_CHUNKS):
        rows = pl.ds(c * rc, rc + HALO)
        gate = conv(accg_scr[rows, :], cwg_ref, cbg_ref)
        up = conv(accu_scr[rows, :], cwu_ref, cbu_ref)
        o_ref[pl.ds(c * rc, rc), :] = (gate / (1.0 + jnp.exp(-gate)) * up).astype(o_ref.dtype)

        n_rows = rc + (HALO if c == FFN_ROW_CHUNKS - 1 else 0)
        lhs = lhs_scr[pl.ds(c * rc, n_rows), :]
        accg_scr[pl.ds(c * rc, n_rows), :] = _dot(lhs, wg_scr[...])
        accu_scr[pl.ds(c * rc, n_rows), :] = _dot(lhs, wu_scr[...])


def _ffn_up(hn, w_up, conv_w, conv_b, layer, seq):
    T, D = hn.shape
    F = w_up.shape[2] // 2
    bm = _pick(seq, 2048, BF16_SUBLANES * FFN_ROW_CHUNKS)
    bn = _pick(F, 256, LANES)
    nb = F // bn
    n_tiles = (T // bm) * nb
    conv_b3 = conv_b.reshape(conv_b.shape[0], 1, 2 * F)
    vmem = ((bm + HALO) * D * 2 + 4 * D * bn * 4 + 2 * D * bn * 2 + 2 * bm * bn * 2
            + 4 * (bm + HALO) * bn * 4)
    body = functools.partial(_ffn_up_body, blocks_per_seq=seq // bm, nb=nb, n_tiles=n_tiles)

    def cur(i):
        return jnp.minimum(i, n_tiles - 1)

    def prev(i):
        return jnp.maximum(i - 1, 0)

    return pl.pallas_call(
        body,
        grid=(n_tiles + 1,),
        in_specs=[pl.BlockSpec(memory_space=pl.ANY),
                  pl.BlockSpec((None, D, bn), lambda i: (layer, 0, cur(i) % nb)),
                  pl.BlockSpec((None, D, bn), lambda i: (layer, 0, cur(i) % nb + nb)),
                  pl.BlockSpec((None, CONV_WIDTH, bn), lambda i: (layer, 0, prev(i) % nb)),
                  pl.BlockSpec((None, CONV_WIDTH, bn), lambda i: (layer, 0, prev(i) % nb + nb)),
                  pl.BlockSpec((None, 1, bn), lambda i: (layer, 0, prev(i) % nb)),
                  pl.BlockSpec((None, 1, bn), lambda i: (layer, 0, prev(i) % nb + nb))],
        out_specs=pl.BlockSpec((bm, bn), lambda i: (prev(i) // nb, prev(i) % nb)),
        out_shape=jax.ShapeDtypeStruct((T, F), BF16),
        scratch_shapes=[pltpu.VMEM((bm + HALO, D), BF16),
                        pltpu.VMEM((D, bn), BF16),
                        pltpu.VMEM((D, bn), BF16),
                        pltpu.VMEM((bm + HALO, bn), F32),
                        pltpu.VMEM((bm + HALO, bn), F32),
                        pltpu.SemaphoreType.DMA(())],
        compiler_params=_params(("arbitrary",), vmem),
        name="ffn_up",
    )(hn, w_up, w_up, conv_w, conv_w, conv_b3, conv_b3)


def _rope_tables(seq):
    pos = jnp.arange(seq, dtype=F32)
    inv = ROPE_THETA ** (-jnp.arange(0, ROPE_DIM, 2, dtype=F32) / ROPE_DIM)
    ang = pos[:, None] * inv[None, :]
    cos, sin = jnp.cos(ang), jnp.sin(ang)
    zeros = jnp.zeros((seq, LANES - ROPE_DIM), F32)
    return (jnp.concatenate([cos, cos, zeros], axis=-1), jnp.concatenate([-sin, sin, zeros], axis=-1))


def _with_swapped_rope(w_rope):
    half = ROPE_DIM // 2
    return jnp.concatenate([w_rope, w_rope[..., half:], w_rope[..., :half]], axis=-1)


def _rope(y, c_tab, s_tab):
    return y * c_tab + pltpu.roll(y, ROPE_DIM, 1) * s_tab


def _kv_down_body(x_ref, w_ref, gl_ref, gr_ref, c_ref, s_ref, ckv_ref, kr_ref, ssq_ref, *, kv_rank):
    acc = _dot(x_ref[...], w_ref[...])
    c = acc[:, :kv_rank]
    ckv_ref[...] = (c * _rms_scale(c, kv_rank) * gl_ref[...]).astype(ckv_ref.dtype)
    kr = acc[:, kv_rank:kv_rank + LANES]
    ssq_ref[...] = jnp.broadcast_to(0.5 * jnp.sum(kr * kr, axis=-1, keepdims=True), kr.shape)
    kr_ref[...] = _rope(kr * gr_ref[...], c_ref[...], s_ref[...])


def _kv_down(hn, w_pad, latent_gain, rope_gain_pad, tables, seq, kv_rank):
    T, D = hn.shape
    N = w_pad.shape[1]
    bm = _pick(seq, 1024, BF16_SUBLANES)
    spb = seq // bm
    vmem = 2 * bm * D * 2 + 2 * D * N * 2 + 4 * bm * N * 4 + 2 * bm * kv_rank * 2 + 10 * bm * LANES * 4
    tab_spec = pl.BlockSpec((bm, LANES), lambda m: (m % spb, 0))
    return pl.pallas_call(
        functools.partial(_kv_down_body, kv_rank=kv_rank),
        grid=(T // bm,),
        in_specs=[pl.BlockSpec((bm, D), lambda m: (m, 0)),
                  pl.BlockSpec((D, N), lambda m: (0, 0)),
                  pl.BlockSpec((1, kv_rank), lambda m: (0, 0)),
                  pl.BlockSpec((1, LANES), lambda m: (0, 0)),
                  tab_spec, tab_spec],
        out_specs=[pl.BlockSpec((bm, kv_rank), lambda m: (m, 0)),
                   pl.BlockSpec((bm, LANES), lambda m: (m, 0)),
                   pl.BlockSpec((bm, LANES), lambda m: (m, 0))],
        out_shape=[jax.ShapeDtypeStruct((T, kv_rank), BF16),
                   jax.ShapeDtypeStruct((T, LANES), F32),
                   jax.ShapeDtypeStruct((T, LANES), F32)],
        compiler_params=_params(("parallel",), vmem),
        name="kv_down",
    )(hn, w_pad, latent_gain, rope_gain_pad, *tables)


def _kv_up_body(c_ref, w_ref, kr_ref, ssq_ref, gn_ref, k_ref, v_ref, *, heads_per_blk):
    acc = _dot(c_ref[...], w_ref[...])
    kr = kr_ref[...]
    ssq = ssq_ref[...]
    width = NOPE_DIM + V_DIM
    for j in range(heads_per_blk):
        kn = acc[:, j * width:j * width + NOPE_DIM]
        f = lax.rsqrt((jnp.sum(kn * kn, axis=-1, keepdims=True) + ssq) / QK_DIM + EPS)
        k_ref[j, :, 0:NOPE_DIM] = (kn * f * gn_ref[...]).astype(k_ref.dtype)
        k_ref[j, :, NOPE_DIM:2 * LANES] = (kr * f).astype(k_ref.dtype)
        v_ref[j, :, 0:V_DIM] = acc[:, j * width + NOPE_DIM:(j + 1) * width].astype(v_ref.dtype)
        v_ref[j, :, V_DIM:2 * LANES] = jnp.ones((acc.shape[0], 2 * LANES - V_DIM), v_ref.dtype)


def _kv_up(ckv, w, kr, ssq, nope_gain, n_heads):
    T, K = ckv.shape
    width = NOPE_DIM + V_DIM
    bm = _pick(T, 1024, BF16_SUBLANES)
    hpb = _pick(n_heads, 4, 1)
    bn = hpb * width
    vmem = 2 * bm * K * 2 + 2 * K * bn * 2 + 4 * bm * LANES * 4 + 2 * bm * bn * 4 + 2 * hpb * bm * 3 * LANES * 2
    return pl.pallas_call(
        functools.partial(_kv_up_body, heads_per_blk=hpb),
        grid=(T // bm, n_heads // hpb),
        in_specs=[pl.BlockSpec((bm, K), lambda m, n: (m, 0)),
                  pl.BlockSpec((K, bn), lambda m, n: (0, n)),
                  pl.BlockSpec((bm, LANES), lambda m, n: (m, 0)),
                  pl.BlockSpec((bm, LANES), lambda m, n: (m, 0)),
                  pl.BlockSpec((1, NOPE_DIM), lambda m, n: (0, 0))],
        out_specs=[pl.BlockSpec((hpb, bm, 2 * LANES), lambda m, n: (n, m, 0)),
                   pl.BlockSpec((hpb, bm, 2 * LANES), lambda m, n: (n, m, 0))],
        out_shape=[jax.ShapeDtypeStruct((n_heads, T, 2 * LANES), BF16),
                   jax.ShapeDtypeStruct((n_heads, T, 2 * LANES), BF16)],
        compiler_params=_params(("parallel", "parallel"), vmem),
        name="kv_up",
    )(ckv, w, kr, ssq, nope_gain)


def _q_down_body(x_ref, w_ref, g_ref, o_ref):
    acc = _dot(x_ref[...], w_ref[...])
    o_ref[...] = (acc * _rms_scale(acc, acc.shape[-1]) * g_ref[...]).astype(o_ref.dtype)


def _q_down(hn, w_all, layer, gain):
    T, D = hn.shape
    N = w_all.shape[2]
    bm = _pick(T, 1024, BF16_SUBLANES)
    vmem = 2 * bm * D * 2 + 2 * D * N * 2 + 3 * bm * N * 4 + 2 * bm * N * 2
    return pl.pallas_call(
        _q_down_body,
        grid=(T // bm,),
        in_specs=[pl.BlockSpec((bm, D), lambda m: (m, 0)),
                  pl.BlockSpec((None, D, N), lambda m: (layer, 0, 0)),
                  pl.BlockSpec((1, N), lambda m: (0, 0))],
        out_specs=pl.BlockSpec((bm, N), lambda m: (m, 0)),
        out_shape=jax.ShapeDtypeStruct((T, N), BF16),
        compiler_params=_params(("parallel",), vmem),
        name="q_down",
    )(hn, w_all, gain)


def _q_up_body(c_ref, w_ref, gn_ref, gr_ref, ct_ref, st_ref, q_ref, acc_scr, *, heads_per_blk):
    @pl.when(pl.program_id(0) == 0)
    def _():
        acc_scr[...] = jnp.zeros(acc_scr.shape, acc_scr.dtype)

    rc = acc_scr.shape[0] // ROW_CHUNKS
    for c in range(ROW_CHUNKS):
        rows = pl.ds(c * rc, rc)
        ct, st = ct_ref[rows, :], st_ref[rows, :]
        for j in range(heads_per_blk):
            tn = acc_scr[rows, 2 * j * LANES:(2 * j + 1) * LANES]
            tr = acc_scr[rows, (2 * j + 1) * LANES:(2 * j + 2) * LANES]
            ssq = jnp.sum(tn * tn + 0.5 * (tr * tr), axis=-1, keepdims=True)
            f = lax.rsqrt(ssq / QK_DIM + EPS)
            q_ref[j, rows, 0:LANES] = (tn * f * gn_ref[...]).astype(q_ref.dtype)
            q_ref[j, rows, LANES:2 * LANES] = (_rope(tr * gr_ref[...], ct, st) * f).astype(q_ref.dtype)
        acc_scr[rows, :] = _dot(c_ref[rows, :], w_ref[...])


def _q_up(cq, w_pad, nope_gain, rope_gain_pad, tables, seq, n_heads):
    T, K = cq.shape
    bm = _pick(seq, 1024, BF16_SUBLANES * ROW_CHUNKS)
    spb = seq // bm
    hpb = _pick(n_heads, 4, 1)
    bn = hpb * 2 * LANES
    nb = n_heads // hpb
    n_tiles = (T // bm) * nb
    vmem = 2 * bm * K * 2 + 2 * K * bn * 2 + 6 * bm * LANES * 4 + 3 * bm * bn * 4 + 2 * bm * bn * 2

    def cur(i):
        return jnp.minimum(i, n_tiles - 1)

    def prev(i):
        return jnp.maximum(i - 1, 0)

    tab_spec = pl.BlockSpec((bm, LANES), lambda i: (prev(i) // nb % spb, 0))
    return pl.pallas_call(
        functools.partial(_q_up_body, heads_per_blk=hpb),
        grid=(n_tiles + 1,),
        in_specs=[pl.BlockSpec((bm, K), lambda i: (cur(i) // nb, 0)),
                  pl.BlockSpec((K, bn), lambda i: (0, cur(i) % nb)),
                  pl.BlockSpec((1, LANES), lambda i: (0, 0)),
                  pl.BlockSpec((1, LANES), lambda i: (0, 0)),
                  tab_spec, tab_spec],
        out_specs=pl.BlockSpec((hpb, bm, 2 * LANES), lambda i: (prev(i) % nb, prev(i) // nb, 0)),
        out_shape=jax.ShapeDtypeStruct((n_heads, T, 2 * LANES), BF16),
        scratch_shapes=[pltpu.VMEM((bm, bn), F32)],
        compiler_params=_params(("arbitrary",), vmem),
        name="q_up",
    )(cq, w_pad, nope_gain, rope_gain_pad, *tables)


def _mla_body(q_ref, k_ref, v_ref, o_ref, *, blk, heads_per_step):
    qi = pl.program_id(2)
    qs = [q_ref[t] for t in range(heads_per_step)]

    def chunk(j, carry, diagonal):
        k0 = pl.multiple_of(j * blk, blk)
        out = []
        for t in range(heads_per_step):
            m, acc = carry[t]
            s = _dot_nt(qs[t], k_ref[t, pl.ds(k0, blk), :])
            if diagonal:
                row = lax.broadcasted_iota(jnp.int32, s.shape, 0)
                col = lax.broadcasted_iota(jnp.int32, s.shape, 1)
                s = jnp.where(col <= row, s, NEG)
            m_new = jnp.maximum(m, jnp.max(s, axis=-1, keepdims=True))
            p = jnp.exp2(s - m_new)
            acc = jnp.exp2(m - m_new) * acc + _dot(p.astype(BF16), v_ref[t, pl.ds(k0, blk), :])
            out.append((m_new, acc))
        return tuple(out)

    init = tuple((jnp.full((blk, 1), NEG, F32), jnp.zeros((blk, 2 * LANES), F32))
                 for _ in range(heads_per_step))
    carry = lax.fori_loop(0, qi, lambda j, c: chunk(j, c, False), init)
    carry = chunk(qi, carry, True)
    for t in range(heads_per_step):
        acc = carry[t][1]
        o_ref[:, t * V_DIM:(t + 1) * V_DIM] = (acc[:, :V_DIM] / acc[:, V_DIM:2 * V_DIM]).astype(o_ref.dtype)


def _mla_attention(q, k, v1, batch, seq):
    n_heads, T, _ = q.shape
    blk = _pick(seq, 512, BF16_SUBLANES)
    nq = seq // blk
    hps = _pick(n_heads, 4, 1)
    vmem = hps * (2 * blk * 2 * LANES * 2 + 2 * seq * 4 * LANES * 2 + 2 * blk * V_DIM * 2 + 6 * blk * blk * 4)
    return pl.pallas_call(
        functools.partial(_mla_body, blk=blk, heads_per_step=hps),
        grid=(batch, n_heads // hps, nq),
        in_specs=[pl.BlockSpec((hps, blk, 2 * LANES), lambda b, h, i: (h, b * nq + i, 0)),
                  pl.BlockSpec((hps, seq, 2 * LANES), lambda b, h, i: (h, b, 0)),
                  pl.BlockSpec((hps, seq, 2 * LANES), lambda b, h, i: (h, b, 0))],
        out_specs=pl.BlockSpec((blk, hps * V_DIM), lambda b, h, i: (b * nq + i, h)),
        out_shape=jax.ShapeDtypeStruct((T, n_heads * V_DIM), BF16),
        compiler_params=_params(("parallel", "parallel", "parallel"), vmem),
        name="mla_attention",
    )(q, k, v1)


def _conv_ffn(h, gain, w_up, conv_w, conv_b, w_down, layer, seq):
    (hn,) = _rmsnorm(h, gain.reshape(1, -1))
    act = _ffn_up(hn, w_up, conv_w, conv_b, layer, seq)
    return _mm_res(act, w_down, layer, h, 512, 512)


def kernel(x, a_attn_norm, a_w_qkv, a_q_norm, a_k_norm, a_w_o, rel_bias, kv_norm, kv_w_down,
           kv_latent_norm, kv_w_up, kv_k_norm, b_attn_norm, b_w_dq, b_q_latent_norm, b_w_uq, b_q_norm,
           b_w_o, ffn_norm, ffn_w_up, ffn_conv_w, ffn_conv_b, ffn_w_down):
    batch, seq, d_model = x.shape
    T = batch * seq
    n_g = len(A_GROUPS)
    n_a_layers = a_w_qkv.shape[0]
    n_b_layers = b_w_dq.shape[0]
    a_heads = a_w_qkv.shape[2] // (3 * n_g * LANES)
    b_heads = b_w_o.shape[1] // V_DIM
    kv_rank = kv_w_down.shape[1] - ROPE_DIM
    tables = _rope_tables(seq)

    a_w_qkv_bf, a_w_o_bf, b_w_dq_bf, b_w_o_bf = (w.astype(BF16) for w in (a_w_qkv, a_w_o, b_w_dq, b_w_o))
    ffn_w_down_bf = ffn_w_down.astype(BF16)

    h = x.reshape(T, d_model)
    k_shared = v_shared = None
    for layer in range(n_a_layers + n_b_layers):
        if layer < n_a_layers:
            (hn,) = _rmsnorm(h, a_attn_norm[layer].reshape(1, -1))
            q_scale = LANES ** -0.5 * math.log2(math.e)
            qkvs = []
            for g, (_, dil) in enumerate(A_GROUPS):
                gains = jnp.stack([a_q_norm[layer, g] * q_scale, a_k_norm[layer, g], jnp.ones((LANES,), F32)])
                qkvs.append(_qkv_proj(hn, a_w_qkv_bf, layer, g, jnp.repeat(gains, a_heads, axis=0),
                                      batch, seq, dil))
            attn = _dilated_attention(qkvs, rel_bias, batch, seq, a_heads)
            h = _mm_res(attn, a_w_o_bf, layer, h, 1024, 512)
        else:
            i = layer - n_a_layers
            if layer == n_a_layers:
                hn_kv, hn_q = _rmsnorm(h, jnp.stack([kv_norm, b_attn_norm[i]]))
                w_kvd = jnp.concatenate([kv_w_down[:, :kv_rank], _with_swapped_rope(kv_w_down[:, kv_rank:])],
                                        axis=1).astype(BF16)
                ckv, kr, ssq = _kv_down(hn_kv, w_kvd, kv_latent_norm.reshape(1, -1),
                                        _with_swapped_rope(kv_k_norm[NOPE_DIM:]).reshape(1, LANES),
                                        tables, seq, kv_rank)
                k_shared, v_shared = _kv_up(ckv, kv_w_up.astype(BF16), kr, ssq,
                                            kv_k_norm[:NOPE_DIM].reshape(1, -1), b_heads)
            else:
                (hn_q,) = _rmsnorm(h, b_attn_norm[i].reshape(1, -1))
            cq = _q_down(hn_q, b_w_dq_bf, i, b_q_latent_norm[i].reshape(1, -1))
            q_rank = b_w_uq.shape[1]
            w_uq = b_w_uq[i].reshape(q_rank, b_heads, QK_DIM)
            w_uq = jnp.concatenate([w_uq[..., :NOPE_DIM], _with_swapped_rope(w_uq[..., NOPE_DIM:])], axis=-1)
            w_uq = w_uq.reshape(q_rank, b_heads * 2 * LANES).astype(BF16)
            q_gain = b_q_norm[i] * (QK_DIM ** -0.5 * math.log2(math.e))
            q = _q_up(cq, w_uq, q_gain[:NOPE_DIM].reshape(1, -1),
                      _with_swapped_rope(q_gain[NOPE_DIM:]).reshape(1, LANES), tables, seq, b_heads)
            attn = _mla_attention(q, k_shared, v_shared, batch, seq)
            h = _mm_res(attn, b_w_o_bf, i, h, 1024, 512)
        h = _conv_ffn(h, ffn_norm[layer], ffn_w_up, ffn_conv_w, ffn_conv_b, ffn_w_down_bf, layer, seq)
    return h.reshape(batch, seq, d_model)
```

```python
import functools
import math

import numpy as np
import jax
import jax.numpy as jnp
from jax import lax
from jax.experimental import pallas as pl
from jax.experimental.pallas import tpu as pltpu

F32 = jnp.float32
BF16 = jnp.bfloat16

EPS = 1e-6
A_GROUPS = ((128, 1), (512, 4), (2048, 16))
N_BUCKETS = 32
REL_MAX_EXACT = N_BUCKETS // 2
REL_MAX_DISTANCE = 2048
NOPE_DIM = 128
ROPE_DIM = 64
V_DIM = 128
QK_DIM = NOPE_DIM + ROPE_DIM
ROPE_THETA = 10000.0
CONV_WIDTH = 3

LANES = 128
BF16_SUBLANES = 16
V7X_VMEM_BYTES = 64 * 1024 * 1024
HALO = BF16_SUBLANES
ATTN_BLOCKS_PER_ITER = 8
ROW_CHUNKS = 4
FFN_ROW_CHUNKS = 4
MLA_KEY_CHUNK = 512
NEG = -1e30


def _pick(dim, pref, align):
    best = None
    for cand in range(align, min(dim, pref) + 1, align):
        if dim % cand == 0:
            best = cand
    assert best is not None, (dim, pref, align)
    return best


def _params(semantics, vmem_bytes):
    limit = min(int(vmem_bytes * 1.25) + (4 << 20), V7X_VMEM_BYTES - (6 << 20))
    return pltpu.CompilerParams(dimension_semantics=semantics, vmem_limit_bytes=limit)


def _dot(a, b):
    return jnp.dot(a, b, preferred_element_type=F32)


def _dot_nt(a, b):
    return lax.dot_general(a, b, (((1,), (1,)), ((), ())), preferred_element_type=F32)


def _rms_scale(t, dim):
    return lax.rsqrt(jnp.sum(t * t, axis=-1, keepdims=True) / dim + EPS)


def _rmsnorm_body(x_ref, g_ref, *o_refs):
    x = x_ref[...]
    y = x * _rms_scale(x, x.shape[-1])
    for i, o_ref in enumerate(o_refs):
        o_ref[...] = (y * g_ref[i:i + 1, :]).astype(o_ref.dtype)


def _rmsnorm(x, gains):
    M, D = x.shape
    n = gains.shape[0]
    br = _pick(M, 512, 8)
    vmem = 2 * br * D * 4 + 2 * n * br * D * 2 + 2 * br * D * 4
    outs = pl.pallas_call(
        _rmsnorm_body,
        grid=(M // br,),
        in_specs=[pl.BlockSpec((br, D), lambda i: (i, 0)),
                  pl.BlockSpec((n, D), lambda i: (0, 0))],
        out_specs=[pl.BlockSpec((br, D), lambda i: (i, 0))] * n,
        out_shape=[jax.ShapeDtypeStruct((M, D), BF16)] * n,
        compiler_params=_params(("parallel",), vmem),
        name="rmsnorm",
    )(x, gains)
    return outs


def _qkv_body(x_ref, w_ref, g_ref, o_ref, acc_scr, *, nb, n_norm_blocks, heads_per_blk, dil):
    i = pl.program_id(0)
    bm = acc_scr.shape[1]
    rc = bm // ROW_CHUNKS

    @pl.when(i == 0)
    def _():
        acc_scr[...] = jnp.zeros(acc_scr.shape, acc_scr.dtype)

    n_prev = jnp.maximum(i - 1, 0) % nb

    def step(normalise):
        gain = g_ref[pl.ds(n_prev * 3 // nb, 1), :]
        for c in range(ROW_CHUNKS):
            for j in range(heads_per_blk):
                if normalise and dil > 1:
                    t = acc_scr[j, pl.ds(c * rc, rc), :]
                    acc_scr[j, pl.ds(c * rc, rc), :] = t * _rms_scale(t, LANES) * gain
                for r in range(dil):
                    t = acc_scr[j, pl.ds(c * rc + r, rc // dil, stride=dil), :]
                    if normalise and dil == 1:
                        t = t * _rms_scale(t, LANES) * gain
                    o_ref[j, pl.ds(c * rc // dil, rc // dil), r * LANES:(r + 1) * LANES] = t.astype(o_ref.dtype)
            res = _dot(x_ref[pl.ds(c * rc, rc), :], w_ref[...])
            for j in range(heads_per_blk):
                acc_scr[j, pl.ds(c * rc, rc), :] = res[:, j * LANES:(j + 1) * LANES]

    prev_normalised = n_prev < n_norm_blocks
    pl.when(prev_normalised)(lambda: step(True))
    pl.when(jnp.logical_not(prev_normalised))(lambda: step(False))


def _qkv_proj(hn, w_all, layer, group, head_gains, batch, seq, dil):
    T, D = hn.shape
    n_g = len(A_GROUPS)
    width = w_all.shape[2] // (3 * n_g)
    bm = _pick(seq, 1024, BF16_SUBLANES * dil * ROW_CHUNKS)
    bn = _pick(width, 1024, LANES)
    hpb = bn // LANES
    spb = seq // bm
    bps = width // bn
    nb = 3 * bps
    n_tiles = (T // bm) * nb
    vmem = 2 * bm * D * 2 + 2 * D * bn * 2 + 2 * bm * bn * 2 + 3 * bm * bn * 4
    body = functools.partial(_qkv_body, nb=nb, n_norm_blocks=2 * bps, heads_per_blk=hpb, dil=dil)

    def cur(i):
        return jnp.minimum(i, n_tiles - 1)

    def prev(i):
        return jnp.maximum(i - 1, 0)

    def w_block(n):
        return (n // bps * n_g + group) * bps + n % bps

    return pl.pallas_call(
        body,
        grid=(n_tiles + 1,),
        in_specs=[pl.BlockSpec((bm, D), lambda i: (cur(i) // nb, 0)),
                  pl.BlockSpec((None, D, bn), lambda i: (layer, 0, w_block(cur(i) % nb))),
                  pl.BlockSpec((3, LANES), lambda i: (0, 0))],
        out_specs=pl.BlockSpec((hpb, None, bm // dil, dil * LANES),
                               lambda i: (prev(i) % nb, prev(i) // nb // spb, prev(i) // nb % spb, 0)),
        out_shape=jax.ShapeDtypeStruct((nb * hpb, batch, seq // dil, dil * LANES), BF16),
        scratch_shapes=[pltpu.VMEM((hpb, bm, LANES), F32)],
        compiler_params=_params(("arbitrary",), vmem),
        name=f"qkv_proj_d{dil}",
    )(hn, w_all, head_gains)


def _t5_bucket_np(dist):
    dist = np.asarray(dist, np.int64)
    d_f = np.maximum(dist, 1).astype(np.float64)
    large = REL_MAX_EXACT + (np.log(d_f / REL_MAX_EXACT) / math.log(REL_MAX_DISTANCE / REL_MAX_EXACT)
                             * (N_BUCKETS - REL_MAX_EXACT)).astype(np.int64)
    large = np.minimum(large, N_BUCKETS - 1)
    return np.where(dist < REL_MAX_EXACT, dist, large)


def _band_buckets():
    out = []
    for window, dilation in A_GROUPS:
        C = window // dilation
        qi = np.arange(C)[:, None]
        kk = np.arange(2 * C)[None, :]
        steps = qi + C - kk
        band = (steps >= 0) & (steps <= C)
        bucket = _t5_bucket_np(np.maximum(steps, 0) * dilation)
        out.append(np.where(band, bucket, -1).astype(np.int32))
    return np.stack(out)


def _dilated_body(tab_ref, bkt_ref, *refs, n_heads, seq):
    n_g = len(A_GROUPS)
    qkv_refs = refs[:3 * n_g]
    o_ref, acc_scr, m_scr = refs[3 * n_g:]
    h = pl.program_id(1)

    for g, (window, dil) in enumerate(A_GROUPS):
        C = window // dil
        q_ref, k_ref, v_ref = qkv_refs[3 * g:3 * g + 3]
        n_blocks = seq // dil // C
        first_group = g == 0

        bkt = bkt_ref[g]
        bias = jnp.full((C, 2 * C), NEG, F32)
        for b in range(N_BUCKETS):
            bias = jnp.where(bkt == b, tab_ref[(g * n_heads + h) * N_BUCKETS + b], bias)

        def attend(c, r, first_block, bias=bias, q_ref=q_ref, k_ref=k_ref, v_ref=v_ref, C=C, dil=dil,
                   first_group=first_group):
            lanes = slice(r * LANES, (r + 1) * LANES)
            if first_block:
                q, kk, vv, bias_blk = q_ref[0:C, lanes], k_ref[0:C, lanes], v_ref[0:C, lanes], bias[:, C:]
            else:
                r0 = c * C if isinstance(c, int) else pl.multiple_of(c * C, C)
                rp = (c - 1) * C if isinstance(c, int) else pl.multiple_of((c - 1) * C, C)
                q, kk, vv, bias_blk = (q_ref[pl.ds(r0, C), lanes], k_ref[pl.ds(rp, 2 * C), lanes],
                                       v_ref[pl.ds(rp, 2 * C), lanes], bias)
            s = _dot_nt(q, kk) + bias_blk
            m = jnp.max(s, axis=-1, keepdims=True)
            p = jnp.exp2(s - m)
            o1 = _dot(p.astype(BF16), jnp.concatenate([vv, jnp.ones_like(vv)], axis=1))

            if dil == 1:
                pos = pl.ds(0 if first_block else r0, C)
            else:
                pos = pl.ds((0 if first_block else c * C) * dil + r, C, stride=dil)
            if first_group:
                acc_scr[0, pos, :] = o1[:, 0:LANES]
                acc_scr[1, pos, :] = o1[:, LANES:2 * LANES]
                m_scr[pos, :] = jnp.broadcast_to(m, (C, LANES))
            else:
                m_old = m_scr[pos, :]
                m_new = jnp.maximum(m_old, m)
                a = jnp.exp2(m_old - m_new)
                b = jnp.exp2(m - m_new)
                acc_scr[0, pos, :] = a * acc_scr[0, pos, :] + b * o1[:, 0:LANES]
                acc_scr[1, pos, :] = a * acc_scr[1, pos, :] + b * o1[:, LANES:2 * LANES]
                m_scr[pos, :] = m_new

        if dil == 1:
            per_iter = min(ATTN_BLOCKS_PER_ITER, n_blocks)
            assert n_blocks % per_iter == 0
            for u in range(per_iter):
                attend(u, 0, u == 0)

            def body(bi, carry, attend=attend, per_iter=per_iter):
                for u in range(per_iter):
                    attend(bi * per_iter + u, 0, False)
                return carry

            lax.fori_loop(1, n_blocks // per_iter, body, 0)
        else:
            for r in range(dil):
                for c in range(n_blocks):
                    attend(c, r, c == 0)

    rows = ATTN_BLOCKS_PER_ITER * LANES
    for i in range(seq // rows):
        sl = slice(i * rows, (i + 1) * rows)
        o_ref[sl, :] = (acc_scr[0, sl, :] / acc_scr[1, sl, :]).astype(o_ref.dtype)


def _dilated_attention(qkvs, rel_bias, batch, seq, n_heads):
    tab = (jnp.transpose(rel_bias, (1, 2, 0)) * math.log2(math.e)).reshape(-1).astype(F32)
    bkt = jnp.asarray(_band_buckets())
    C = A_GROUPS[0][0] // A_GROUPS[0][1]

    in_specs = [pl.BlockSpec(memory_space=pltpu.SMEM),
                pl.BlockSpec(bkt.shape, lambda b, h: (0, 0, 0))]
    args = [tab, bkt]
    for g, (window, dil) in enumerate(A_GROUPS):
        assert window // dil == C and seq % (dil * C) == 0
        for s in range(3):
            in_specs.append(pl.BlockSpec((None, None, seq // dil, dil * LANES),
                                         lambda b, h, s=s: (s * n_heads + h, b, 0, 0)))
            args.append(qkvs[g])
    vmem = len(A_GROUPS) * 3 * 2 * seq * LANES * 2 + 2 * seq * LANES * 2 + seq * 3 * LANES * 4
    return pl.pallas_call(
        functools.partial(_dilated_body, n_heads=n_heads, seq=seq),
        grid=(batch, n_heads),
        in_specs=in_specs,
        out_specs=pl.BlockSpec((seq, LANES), lambda b, h: (b, h)),
        out_shape=jax.ShapeDtypeStruct((batch * seq, n_heads * LANES), BF16),
        scratch_shapes=[pltpu.VMEM((2, seq, LANES), F32), pltpu.VMEM((seq, LANES), F32)],
        compiler_params=_params(("parallel", "parallel"), vmem),
        name="dilated_attention",
    )(*args)


def _mm_res_body(x_ref, w_ref, res_ref, out_ref):
    out_ref[...] = res_ref[...] + _dot(x_ref[...], w_ref[...])


def _mm_res(x, w_all, layer, res, bm_pref, bn_pref):
    M, K = x.shape
    N = w_all.shape[2]
    bm = _pick(M, bm_pref, BF16_SUBLANES)
    bn = _pick(N, bn_pref, LANES)
    vmem = 2 * bm * K * 2 + 2 * K * bn * 2 + 5 * bm * bn * 4
    if K > N:
        grid = (N // bn, M // bm)
        mn = lambda a, b: (b, a)
    else:
        grid = (M // bm, N // bn)
        mn = lambda a, b: (a, b)
    return pl.pallas_call(
        _mm_res_body,
        grid=grid,
        in_specs=[pl.BlockSpec((bm, K), lambda a, b: (mn(a, b)[0], 0)),
                  pl.BlockSpec((None, K, bn), lambda a, b: (layer, 0, mn(a, b)[1])),
                  pl.BlockSpec((bm, bn), lambda a, b: mn(a, b))],
        out_specs=pl.BlockSpec((bm, bn), lambda a, b: mn(a, b)),
        out_shape=jax.ShapeDtypeStruct((M, N), F32),
        compiler_params=_params(("parallel", "parallel"), vmem),
        name="mm_res",
    )(x, w_all, res)


def _ffn_up_body(x_hbm, wg_ref, wu_ref, cp_ref, o_ref,
                 lhs_scr, w_scr, accg_scr, accu_scr, lhs_sem, *, blocks_per_seq, nb, n_tiles):
    i = pl.program_id(0)
    m = i // nb
    n = i % nb
    bm = o_ref.shape[0]
    first_of_seq = (m % blocks_per_seq) == 0
    new_rows = (n == 0) & (i < n_tiles)

    @pl.when(i == 0)
    def _():
        accg_scr[...] = jnp.zeros(accg_scr.shape, accg_scr.dtype)
        accu_scr[...] = jnp.zeros(accu_scr.shape, accu_scr.dtype)

    def lhs_copy(src_start, n_rows, dst_start):
        return pltpu.make_async_copy(x_hbm.at[pl.ds(src_start, n_rows), :],
                                     lhs_scr.at[pl.ds(dst_start, n_rows), :], lhs_sem)

    @pl.when(new_rows & first_of_seq)
    def _():
        copy = lhs_copy(pl.multiple_of(m * bm, bm), bm, HALO)
        copy.start()
        lhs_scr[:HALO, :] = jnp.zeros((HALO, lhs_scr.shape[1]), lhs_scr.dtype)
        copy.wait()

    @pl.when(new_rows & jnp.logical_not(first_of_seq))
    def _():
        copy = lhs_copy(pl.multiple_of(m * bm - HALO, HALO), bm + HALO, 0)
        copy.start()
        copy.wait()

    bn = o_ref.shape[1]
    w_scr[:, 0:bn] = wg_ref[...].astype(w_scr.dtype)
    w_scr[:, bn:2 * bn] = wu_ref[...].astype(w_scr.dtype)

    def conv(acc, cp):
        a1 = pltpu.roll(acc, 1, 0)
        a2 = pltpu.roll(acc, 2, 0)
        return acc[HALO:] * cp[2:3, :] + a1[HALO:] * cp[1:2, :] + a2[HALO:] * cp[0:1, :] + cp[3:4, :]

    n_prev = jnp.maximum(i - 1, 0) % nb
    cp_gate = cp_ref[n_prev]
    cp_up = cp_ref[n_prev + nb]

    rc = bm // FFN_ROW_CHUNKS
    for c in range(FFN_ROW_CHUNKS):
        rows = pl.ds(c * rc, rc + HALO)
        gate = conv(accg_scr[rows, :], cp_gate)
        up = conv(accu_scr[rows, :], cp_up)
        o_ref[pl.ds(c * rc, rc), :] = (gate / (1.0 + jnp.exp(-gate)) * up).astype(o_ref.dtype)

        n_rows = rc + (HALO if c == FFN_ROW_CHUNKS - 1 else 0)
        lhs = lhs_scr[pl.ds(c * rc, n_rows), :]
        res = _dot(lhs, w_scr[...])
        accg_scr[pl.ds(c * rc, n_rows), :] = res[:, 0:bn]
        accu_scr[pl.ds(c * rc, n_rows), :] = res[:, bn:2 * bn]


def _ffn_up(hn, w_up, conv_w, conv_b, layer, seq):
    T, D = hn.shape
    n_layers = w_up.shape[0]
    F = w_up.shape[2] // 2
    bm = _pick(seq, 2048, BF16_SUBLANES * FFN_ROW_CHUNKS)
    bn = _pick(F, 256, LANES)
    nb = F // bn
    n_tiles = (T // bm) * nb
    conv_params = jnp.concatenate([conv_w, conv_b[:, None, :]], axis=1)
    conv_params = conv_params.reshape(n_layers, CONV_WIDTH + 1, 2 * nb, bn).transpose(0, 2, 1, 3)
    vmem = ((bm + HALO) * D * 2 + 4 * D * bn * 4 + 2 * D * bn * 2 + 2 * bm * bn * 2
            + 4 * (bm + HALO) * bn * 4 + 4 * nb * 8 * bn * 4)
    body = functools.partial(_ffn_up_body, blocks_per_seq=seq // bm, nb=nb, n_tiles=n_tiles)

    def cur(i):
        return jnp.minimum(i, n_tiles - 1)

    def prev(i):
        return jnp.maximum(i - 1, 0)

    return pl.pallas_call(
        body,
        grid=(n_tiles + 1,),
        in_specs=[pl.BlockSpec(memory_space=pl.ANY),
                  pl.BlockSpec((None, D, bn), lambda i: (layer, 0, cur(i) % nb)),
                  pl.BlockSpec((None, D, bn), lambda i: (layer, 0, cur(i) % nb + nb)),
                  pl.BlockSpec((None, 2 * nb, CONV_WIDTH + 1, bn), lambda i: (layer, 0, 0, 0))],
        out_specs=pl.BlockSpec((bm, bn), lambda i: (prev(i) // nb, prev(i) % nb)),
        out_shape=jax.ShapeDtypeStruct((T, F), BF16),
        scratch_shapes=[pltpu.VMEM((bm + HALO, D), BF16),
                        pltpu.VMEM((D, 2 * bn), BF16),
                        pltpu.VMEM((bm + HALO, bn), F32),
                        pltpu.VMEM((bm + HALO, bn), F32),
                        pltpu.SemaphoreType.DMA(())],
        compiler_params=_params(("arbitrary",), vmem),
        name="ffn_up",
    )(hn, w_up, w_up, conv_params)


def _rope_tables(seq):
    pos = jnp.arange(seq, dtype=F32)
    inv = ROPE_THETA ** (-jnp.arange(0, ROPE_DIM, 2, dtype=F32) / ROPE_DIM)
    ang = pos[:, None] * inv[None, :]
    cos, sin = jnp.cos(ang), jnp.sin(ang)
    zeros = jnp.zeros((seq, LANES - ROPE_DIM), F32)
    return (jnp.concatenate([cos, cos, zeros], axis=-1), jnp.concatenate([-sin, sin, zeros], axis=-1))


def _with_swapped_rope(w_rope):
    half = ROPE_DIM // 2
    return jnp.concatenate([w_rope, w_rope[..., half:], w_rope[..., :half]], axis=-1)


def _rope(y, c_tab, s_tab):
    return y * c_tab + pltpu.roll(y, ROPE_DIM, 1) * s_tab


def _latent_down_body(x_ref, g_ref, wkv_ref, gl_ref, gr_ref, c_ref, s_ref, wq_ref, gq_ref,
                      ckv_ref, kr_ref, ssq_ref, cq_ref, *, kv_rank):
    x = x_ref[...]
    y = x * _rms_scale(x, x.shape[-1])

    acc = _dot((y * g_ref[0:1, :]).astype(BF16), wkv_ref[...])
    c = acc[:, :kv_rank]
    ckv_ref[...] = (c * _rms_scale(c, kv_rank) * gl_ref[...]).astype(ckv_ref.dtype)
    kr = acc[:, kv_rank:kv_rank + LANES]
    ssq_ref[...] = jnp.broadcast_to(0.5 * jnp.sum(kr * kr, axis=-1, keepdims=True), kr.shape)
    kr_ref[...] = _rope(kr * gr_ref[...], c_ref[...], s_ref[...])

    acc_q = _dot((y * g_ref[1:2, :]).astype(BF16), wq_ref[...])
    cq_ref[...] = (acc_q * _rms_scale(acc_q, acc_q.shape[-1]) * gq_ref[...]).astype(cq_ref.dtype)


def _latent_down(h, gains, w_kv, latent_gain, rope_gain_pad, tables, w_dq_all, layer, q_latent_gain, seq, kv_rank):
    T, D = h.shape
    n_kv = w_kv.shape[1]
    q_rank = w_dq_all.shape[2]
    bm = _pick(seq, 256, BF16_SUBLANES)
    spb = seq // bm
    vmem = 4 * bm * D * 4 + 2 * bm * D * 2 + 2 * D * (n_kv + q_rank) * 2 + 6 * bm * (n_kv + q_rank) * 4
    tab_spec = pl.BlockSpec((bm, LANES), lambda m: (m % spb, 0))
    return pl.pallas_call(
        functools.partial(_latent_down_body, kv_rank=kv_rank),
        grid=(T // bm,),
        in_specs=[pl.BlockSpec((bm, D), lambda m: (m, 0)),
                  pl.BlockSpec((2, D), lambda m: (0, 0)),
                  pl.BlockSpec((D, n_kv), lambda m: (0, 0)),
                  pl.BlockSpec((1, kv_rank), lambda m: (0, 0)),
                  pl.BlockSpec((1, LANES), lambda m: (0, 0)),
                  tab_spec, tab_spec,
                  pl.BlockSpec((None, D, q_rank), lambda m: (layer, 0, 0)),
                  pl.BlockSpec((1, q_rank), lambda m: (0, 0))],
        out_specs=[pl.BlockSpec((bm, kv_rank), lambda m: (m, 0)),
                   pl.BlockSpec((bm, LANES), lambda m: (m, 0)),
                   pl.BlockSpec((bm, LANES), lambda m: (m, 0)),
                   pl.BlockSpec((bm, q_rank), lambda m: (m, 0))],
        out_shape=[jax.ShapeDtypeStruct((T, kv_rank), BF16),
                   jax.ShapeDtypeStruct((T, LANES), F32),
                   jax.ShapeDtypeStruct((T, LANES), F32),
                   jax.ShapeDtypeStruct((T, q_rank), BF16)],
        compiler_params=_params(("parallel",), vmem),
        name="latent_down",
    )(h, gains, w_kv, latent_gain, rope_gain_pad, *tables, w_dq_all, q_latent_gain)


def _kv_up_body(c_ref, w_ref, kr_ref, ssq_ref, gn_ref, k_ref, v_ref, *, heads_per_blk):
    acc = _dot(c_ref[...], w_ref[...])
    kr = kr_ref[...]
    ssq = ssq_ref[...]
    width = NOPE_DIM + V_DIM
    for j in range(heads_per_blk):
        kn = acc[:, j * width:j * width + NOPE_DIM]
        f = lax.rsqrt((jnp.sum(kn * kn, axis=-1, keepdims=True) + ssq) / QK_DIM + EPS)
        k_ref[j, :, 0:NOPE_DIM] = (kn * f * gn_ref[...]).astype(k_ref.dtype)
        k_ref[j, :, NOPE_DIM:2 * LANES] = (kr * f).astype(k_ref.dtype)
        v_ref[j, :, 0:V_DIM] = acc[:, j * width + NOPE_DIM:(j + 1) * width].astype(v_ref.dtype)
        v_ref[j, :, V_DIM:2 * LANES] = jnp.ones((acc.shape[0], 2 * LANES - V_DIM), v_ref.dtype)


def _kv_up(ckv, w, kr, ssq, nope_gain, n_heads):
    T, K = ckv.shape
    width = NOPE_DIM + V_DIM
    bm = _pick(T, 1024, BF16_SUBLANES)
    hpb = _pick(n_heads, 8, 1)
    bn = hpb * width
    vmem = 2 * bm * K * 2 + 2 * K * bn * 2 + 4 * bm * LANES * 4 + 2 * bm * bn * 4 + 2 * hpb * bm * 4 * LANES * 2
    return pl.pallas_call(
        functools.partial(_kv_up_body, heads_per_blk=hpb),
        grid=(T // bm, n_heads // hpb),
        in_specs=[pl.BlockSpec((bm, K), lambda m, n: (m, 0)),
                  pl.BlockSpec((K, bn), lambda m, n: (0, n)),
                  pl.BlockSpec((bm, LANES), lambda m, n: (m, 0)),
                  pl.BlockSpec((bm, LANES), lambda m, n: (m, 0)),
                  pl.BlockSpec((1, NOPE_DIM), lambda m, n: (0, 0))],
        out_specs=[pl.BlockSpec((hpb, bm, 2 * LANES), lambda m, n: (n, m, 0)),
                   pl.BlockSpec((hpb, bm, 2 * LANES), lambda m, n: (n, m, 0))],
        out_shape=[jax.ShapeDtypeStruct((n_heads, T, 2 * LANES), BF16),
                   jax.ShapeDtypeStruct((n_heads, T, 2 * LANES), BF16)],
        compiler_params=_params(("parallel", "parallel"), vmem),
        name="kv_up",
    )(ckv, w, kr, ssq, nope_gain)


def _q_down_body(x_ref, w_ref, g_ref, o_ref):
    acc = _dot(x_ref[...], w_ref[...])
    o_ref[...] = (acc * _rms_scale(acc, acc.shape[-1]) * g_ref[...]).astype(o_ref.dtype)


def _q_down(hn, w_all, layer, gain):
    T, D = hn.shape
    N = w_all.shape[2]
    bm = _pick(T, 1024, BF16_SUBLANES)
    vmem = 2 * bm * D * 2 + 2 * D * N * 2 + 3 * bm * N * 4 + 2 * bm * N * 2
    return pl.pallas_call(
        _q_down_body,
        grid=(T // bm,),
        in_specs=[pl.BlockSpec((bm, D), lambda m: (m, 0)),
                  pl.BlockSpec((None, D, N), lambda m: (layer, 0, 0)),
                  pl.BlockSpec((1, N), lambda m: (0, 0))],
        out_specs=pl.BlockSpec((bm, N), lambda m: (m, 0)),
        out_shape=jax.ShapeDtypeStruct((T, N), BF16),
        compiler_params=_params(("parallel",), vmem),
        name="q_down",
    )(hn, w_all, gain)


def _q_up_body(c_ref, w_ref, gn_ref, gr_ref, ct_ref, st_ref, q_ref, acc_scr, *, heads_per_blk):
    @pl.when(pl.program_id(0) == 0)
    def _():
        acc_scr[...] = jnp.zeros(acc_scr.shape, acc_scr.dtype)

    rc = acc_scr.shape[0] // ROW_CHUNKS
    for c in range(ROW_CHUNKS):
        rows = pl.ds(c * rc, rc)
        ct, st = ct_ref[rows, :], st_ref[rows, :]
        for j in range(heads_per_blk):
            tn = acc_scr[rows, 2 * j * LANES:(2 * j + 1) * LANES]
            tr = acc_scr[rows, (2 * j + 1) * LANES:(2 * j + 2) * LANES]
            ssq = jnp.sum(tn * tn + 0.5 * (tr * tr), axis=-1, keepdims=True)
            f = lax.rsqrt(ssq / QK_DIM + EPS)
            q_ref[j, rows, 0:LANES] = (tn * f * gn_ref[...]).astype(q_ref.dtype)
            q_ref[j, rows, LANES:2 * LANES] = (_rope(tr * gr_ref[...], ct, st) * f).astype(q_ref.dtype)
        acc_scr[rows, :] = _dot(c_ref[rows, :], w_ref[...])


def _q_up(cq, w_pad, nope_gain, rope_gain_pad, tables, seq, n_heads):
    T, K = cq.shape
    bm = _pick(seq, 1024, BF16_SUBLANES * ROW_CHUNKS)
    spb = seq // bm
    hpb = _pick(n_heads, 8, 1)
    bn = hpb * 2 * LANES
    nb = n_heads // hpb
    n_tiles = (T // bm) * nb
    vmem = 2 * bm * K * 2 + 2 * K * bn * 2 + 6 * bm * LANES * 4 + 3 * bm * bn * 4 + 2 * bm * bn * 2

    def cur(i):
        return jnp.minimum(i, n_tiles - 1)

    def prev(i):
        return jnp.maximum(i - 1, 0)

    tab_spec = pl.BlockSpec((bm, LANES), lambda i: (prev(i) // nb % spb, 0))
    return pl.pallas_call(
        functools.partial(_q_up_body, heads_per_blk=hpb),
        grid=(n_tiles + 1,),
        in_specs=[pl.BlockSpec((bm, K), lambda i: (cur(i) // nb, 0)),
                  pl.BlockSpec((K, bn), lambda i: (0, cur(i) % nb)),
                  pl.BlockSpec((1, LANES), lambda i: (0, 0)),
                  pl.BlockSpec((1, LANES), lambda i: (0, 0)),
                  tab_spec, tab_spec],
        out_specs=pl.BlockSpec((hpb, bm, 2 * LANES), lambda i: (prev(i) % nb, prev(i) // nb, 0)),
        out_shape=jax.ShapeDtypeStruct((n_heads, T, 2 * LANES), BF16),
        scratch_shapes=[pltpu.VMEM((bm, bn), F32)],
        compiler_params=_params(("arbitrary",), vmem),
        name="q_up",
    )(cq, w_pad, nope_gain, rope_gain_pad, *tables)


def _mla_body(q_ref, k_ref, v_ref, o_ref, *, blk, bk, heads_per_step):
    qi = pl.program_id(2)
    qs = [q_ref[t] for t in range(heads_per_step)]

    def chunk(j, carry, diag_offset=None):
        k0 = pl.multiple_of(j * bk, bk)

        ss = [_dot_nt(qs[t], k_ref[t, pl.ds(k0, bk), :]) for t in range(heads_per_step)]
        if diag_offset is not None:
            row = lax.broadcasted_iota(jnp.int32, ss[0].shape, 0)
            col = lax.broadcasted_iota(jnp.int32, ss[0].shape, 1) + diag_offset
            ss = [jnp.where(col <= row, s, NEG) for s in ss]
        ms = [jnp.maximum(carry[t][0], jnp.max(ss[t], axis=-1, keepdims=True)) for t in range(heads_per_step)]
        ps = [jnp.exp2(ss[t] - ms[t]).astype(BF16) for t in range(heads_per_step)]
        out = []
        for t in range(heads_per_step):
            m, acc = carry[t]
            acc = jnp.exp2(m - ms[t]) * acc + _dot(ps[t], v_ref[t, pl.ds(k0, bk), :])
            out.append((ms[t], acc))
        return tuple(out)

    per_blk = blk // bk
    carry = tuple((jnp.full((blk, 1), NEG, F32), jnp.zeros((blk, 2 * LANES), F32))
                  for _ in range(heads_per_step))
    carry = lax.fori_loop(0, qi * per_blk, chunk, carry)
    for d in range(per_blk):
        carry = chunk(qi * per_blk + d, carry, diag_offset=d * bk)
    for t in range(heads_per_step):
        acc = carry[t][1]
        o_ref[:, t * V_DIM:(t + 1) * V_DIM] = (acc[:, :V_DIM] / acc[:, V_DIM:2 * V_DIM]).astype(o_ref.dtype)


def _mla_attention(q, k, v1, batch, seq):
    n_heads, T, _ = q.shape
    blk = _pick(seq, 512, BF16_SUBLANES)
    nq = seq // blk
    hps = _pick(n_heads, 4, 1)
    vmem = hps * (2 * blk * 2 * LANES * 2 + 2 * seq * 4 * LANES * 2 + 2 * blk * V_DIM * 2 + 6 * blk * blk * 4)
    return pl.pallas_call(
        functools.partial(_mla_body, blk=blk, bk=_pick(blk, MLA_KEY_CHUNK, BF16_SUBLANES), heads_per_step=hps),
        grid=(batch, n_heads // hps, nq),
        in_specs=[pl.BlockSpec((hps, blk, 2 * LANES), lambda b, h, i: (h, b * nq + i, 0)),
                  pl.BlockSpec((hps, seq, 2 * LANES), lambda b, h, i: (h, b, 0)),
                  pl.BlockSpec((hps, seq, 2 * LANES), lambda b, h, i: (h, b, 0))],
        out_specs=pl.BlockSpec((blk, hps * V_DIM), lambda b, h, i: (b * nq + i, h)),
        out_shape=jax.ShapeDtypeStruct((T, n_heads * V_DIM), BF16),
        compiler_params=_params(("parallel", "parallel", "parallel"), vmem),
        name="mla_attention",
    )(q, k, v1)


def _conv_ffn(h, gain, w_up, conv_w, conv_b, w_down, layer, seq):
    (hn,) = _rmsnorm(h, gain.reshape(1, -1))
    act = _ffn_up(hn, w_up, conv_w, conv_b, layer, seq)
    return _mm_res(act, w_down, layer, h, 512, 512)


def kernel(x, a_attn_norm, a_w_qkv, a_q_norm, a_k_norm, a_w_o, rel_bias, kv_norm, kv_w_down,
           kv_latent_norm, kv_w_up, kv_k_norm, b_attn_norm, b_w_dq, b_q_latent_norm, b_w_uq, b_q_norm,
           b_w_o, ffn_norm, ffn_w_up, ffn_conv_w, ffn_conv_b, ffn_w_down):
    batch, seq, d_model = x.shape
    T = batch * seq
    n_g = len(A_GROUPS)
    n_a_layers = a_w_qkv.shape[0]
    n_b_layers = b_w_dq.shape[0]
    a_heads = a_w_qkv.shape[2] // (3 * n_g * LANES)
    b_heads = b_w_o.shape[1] // V_DIM
    kv_rank = kv_w_down.shape[1] - ROPE_DIM
    tables = _rope_tables(seq)

    a_w_qkv_bf, a_w_o_bf, b_w_dq_bf, b_w_o_bf = (w.astype(BF16) for w in (a_w_qkv, a_w_o, b_w_dq, b_w_o))
    ffn_w_down_bf = ffn_w_down.astype(BF16)

    h = x.reshape(T, d_model)
    k_shared = v_shared = None
    for layer in range(n_a_layers + n_b_layers):
        if layer < n_a_layers:
            (hn,) = _rmsnorm(h, a_attn_norm[layer].reshape(1, -1))
            q_scale = LANES ** -0.5 * math.log2(math.e)
            qkvs = []
            for g, (_, dil) in enumerate(A_GROUPS):
                gains = jnp.stack([a_q_norm[layer, g] * q_scale, a_k_norm[layer, g], jnp.ones((LANES,), F32)])
                qkvs.append(_qkv_proj(hn, a_w_qkv_bf, layer, g, gains, batch, seq, dil))
            attn = _dilated_attention(qkvs, rel_bias, batch, seq, a_heads)
            h = _mm_res(attn, a_w_o_bf, layer, h, 1024, 512)
        else:
            i = layer - n_a_layers
            if layer == n_a_layers:
                w_kvd = jnp.concatenate([kv_w_down[:, :kv_rank], _with_swapped_rope(kv_w_down[:, kv_rank:])],
                                        axis=1).astype(BF16)
                ckv, kr, ssq, cq = _latent_down(
                    h, jnp.stack([kv_norm, b_attn_norm[i]]), w_kvd, kv_latent_norm.reshape(1, -1),
                    _with_swapped_rope(kv_k_norm[NOPE_DIM:]).reshape(1, LANES), tables,
                    b_w_dq_bf, i, b_q_latent_norm[i].reshape(1, -1), seq, kv_rank)
                k_shared, v_shared = _kv_up(ckv, kv_w_up.astype(BF16), kr, ssq,
                                            kv_k_norm[:NOPE_DIM].reshape(1, -1), b_heads)
            else:
                (hn_q,) = _rmsnorm(h, b_attn_norm[i].reshape(1, -1))
                cq = _q_down(hn_q, b_w_dq_bf, i, b_q_latent_norm[i].reshape(1, -1))
            q_rank = b_w_uq.shape[1]
            w_uq = b_w_uq[i].reshape(q_rank, b_heads, QK_DIM)
            w_uq = jnp.concatenate([w_uq[..., :NOPE_DIM], _with_swapped_rope(w_uq[..., NOPE_DIM:])], axis=-1)
            w_uq = w_uq.reshape(q_rank, b_heads * 2 * LANES).astype(BF16)
            q_gain = b_q_norm[i] * (QK_DIM ** -0.5 * math.log2(math.e))
            q = _q_up(cq, w_uq, q_gain[:NOPE_DIM].reshape(1, -1),
                      _with_swapped_rope(q_gain[NOPE_DIM:]).reshape(1, LANES), tables, seq, b_heads)
            attn = _mla_attention(q, k_shared, v_shared, batch, seq)
            h = _mm_res(attn, b_w_o_bf, i, h, 1024, 512)
        h = _conv_ffn(h, ffn_norm[layer], ffn_w_up, ffn_conv_w, ffn_conv_b, ffn_w_down_bf, layer, seq)
    return h.reshape(batch, seq, d_model)
```

```python
import functools
import math

import numpy as np
import jax
import jax.numpy as jnp
from jax import lax
from jax.experimental import pallas as pl
from jax.experimental.pallas import tpu as pltpu

F32 = jnp.float32
BF16 = jnp.bfloat16

EPS = 1e-6
A_GROUPS = ((128, 1), (512, 4), (2048, 16))
N_BUCKETS = 32
REL_MAX_EXACT = N_BUCKETS // 2
REL_MAX_DISTANCE = 2048
NOPE_DIM = 128
ROPE_DIM = 64
V_DIM = 128
QK_DIM = NOPE_DIM + ROPE_DIM
ROPE_THETA = 10000.0
CONV_WIDTH = 3

LANES = 128
BF16_SUBLANES = 16
V7X_VMEM_BYTES = 64 * 1024 * 1024
HALO = BF16_SUBLANES
ATTN_BLOCKS_PER_ITER = 8
ROW_CHUNKS = 4
FFN_ROW_CHUNKS = 4
MLA_KEY_CHUNK = 512
RESIDENT_WEIGHT_BYTES = 16 * 1024 * 1024
NEG = -1e30


def _pick(dim, pref, align):
    best = None
    for cand in range(align, min(dim, pref) + 1, align):
        if dim % cand == 0:
            best = cand
    assert best is not None, (dim, pref, align)
    return best


def _params(semantics, vmem_bytes):
    limit = min(int(vmem_bytes * 1.25) + (4 << 20), V7X_VMEM_BYTES - (6 << 20))
    return pltpu.CompilerParams(dimension_semantics=semantics, vmem_limit_bytes=limit)


def _dot(a, b):
    return jnp.dot(a, b, preferred_element_type=F32)


def _dot_nt(a, b):
    return lax.dot_general(a, b, (((1,), (1,)), ((), ())), preferred_element_type=F32)


def _rms_scale(t, dim):
    return lax.rsqrt(jnp.sum(t * t, axis=-1, keepdims=True) / dim + EPS)


def _rmsnorm_body(x_ref, g_ref, *o_refs):
    x = x_ref[...]
    y = x * _rms_scale(x, x.shape[-1])
    for i, o_ref in enumerate(o_refs):
        o_ref[...] = (y * g_ref[i:i + 1, :]).astype(o_ref.dtype)


def _rmsnorm(x, gains):
    M, D = x.shape
    n = gains.shape[0]
    br = _pick(M, 512, 8)
    vmem = 2 * br * D * 4 + 2 * n * br * D * 2 + 2 * br * D * 4
    outs = pl.pallas_call(
        _rmsnorm_body,
        grid=(M // br,),
        in_specs=[pl.BlockSpec((br, D), lambda i: (i, 0)),
                  pl.BlockSpec((n, D), lambda i: (0, 0))],
        out_specs=[pl.BlockSpec((br, D), lambda i: (i, 0))] * n,
        out_shape=[jax.ShapeDtypeStruct((M, D), BF16)] * n,
        compiler_params=_params(("parallel",), vmem),
        name="rmsnorm",
    )(x, gains)
    return outs


def _qkv_body(x_ref, w_ref, g_ref, o_ref, acc_scr, *, nb, n_norm_blocks, heads_per_blk, dil):
    i = pl.program_id(0)
    bm = acc_scr.shape[1]
    rc = bm // ROW_CHUNKS

    @pl.when(i == 0)
    def _():
        acc_scr[...] = jnp.zeros(acc_scr.shape, acc_scr.dtype)

    n_prev = jnp.maximum(i - 1, 0) % nb

    def step(normalise):
        gain = g_ref[pl.ds(n_prev * 3 // nb, 1), :]
        for c in range(ROW_CHUNKS):
            for j in range(heads_per_blk):
                if normalise and dil > 1:
                    t = acc_scr[j, pl.ds(c * rc, rc), :]
                    acc_scr[j, pl.ds(c * rc, rc), :] = t * _rms_scale(t, LANES) * gain
                for r in range(dil):
                    t = acc_scr[j, pl.ds(c * rc + r, rc // dil, stride=dil), :]
                    if normalise and dil == 1:
                        t = t * _rms_scale(t, LANES) * gain
                    o_ref[j, pl.ds(c * rc // dil, rc // dil), r * LANES:(r + 1) * LANES] = t.astype(o_ref.dtype)
            res = _dot(x_ref[pl.ds(c * rc, rc), :], w_ref[...])
            for j in range(heads_per_blk):
                acc_scr[j, pl.ds(c * rc, rc), :] = res[:, j * LANES:(j + 1) * LANES]

    prev_normalised = n_prev < n_norm_blocks
    pl.when(prev_normalised)(lambda: step(True))
    pl.when(jnp.logical_not(prev_normalised))(lambda: step(False))


def _qkv_proj(hn, w_all, layer, group, head_gains, batch, seq, dil):
    T, D = hn.shape
    n_g = len(A_GROUPS)
    width = w_all.shape[2] // (3 * n_g)
    bm = _pick(seq, 1024, BF16_SUBLANES * dil * ROW_CHUNKS)
    bn = _pick(width, 1024, LANES)
    hpb = bn // LANES
    spb = seq // bm
    bps = width // bn
    nb = 3 * bps
    n_tiles = (T // bm) * nb
    vmem = 2 * bm * D * 2 + 2 * D * bn * 2 + 2 * bm * bn * 2 + 3 * bm * bn * 4
    body = functools.partial(_qkv_body, nb=nb, n_norm_blocks=2 * bps, heads_per_blk=hpb, dil=dil)

    def cur(i):
        return jnp.minimum(i, n_tiles - 1)

    def prev(i):
        return jnp.maximum(i - 1, 0)

    def w_block(n):
        return (n // bps * n_g + group) * bps + n % bps

    return pl.pallas_call(
        body,
        grid=(n_tiles + 1,),
        in_specs=[pl.BlockSpec((bm, D), lambda i: (cur(i) // nb, 0)),
                  pl.BlockSpec((None, D, bn), lambda i: (layer, 0, w_block(cur(i) % nb))),
                  pl.BlockSpec((3, LANES), lambda i: (0, 0))],
        out_specs=pl.BlockSpec((hpb, None, bm // dil, dil * LANES),
                               lambda i: (prev(i) % nb, prev(i) // nb // spb, prev(i) // nb % spb, 0)),
        out_shape=jax.ShapeDtypeStruct((nb * hpb, batch, seq // dil, dil * LANES), BF16),
        scratch_shapes=[pltpu.VMEM((hpb, bm, LANES), F32)],
        compiler_params=_params(("arbitrary",), vmem),
        name=f"qkv_proj_d{dil}",
    )(hn, w_all, head_gains)


def _t5_bucket_np(dist):
    dist = np.asarray(dist, np.int64)
    d_f = np.maximum(dist, 1).astype(np.float64)
    large = REL_MAX_EXACT + (np.log(d_f / REL_MAX_EXACT) / math.log(REL_MAX_DISTANCE / REL_MAX_EXACT)
                             * (N_BUCKETS - REL_MAX_EXACT)).astype(np.int64)
    large = np.minimum(large, N_BUCKETS - 1)
    return np.where(dist < REL_MAX_EXACT, dist, large)


def _band_buckets():
    out = []
    for window, dilation in A_GROUPS:
        C = window // dilation
        qi = np.arange(C)[:, None]
        kk = np.arange(2 * C)[None, :]
        steps = qi + C - kk
        band = (steps >= 0) & (steps <= C)
        bucket = _t5_bucket_np(np.maximum(steps, 0) * dilation)
        out.append(np.where(band, bucket, -1).astype(np.int32))
    return np.stack(out)


def _dilated_body(tab_ref, bkt_ref, *refs, n_heads, seq):
    n_g = len(A_GROUPS)
    qkv_refs = refs[:3 * n_g]
    o_ref, acc_scr, m_scr = refs[3 * n_g:]
    h = pl.program_id(1)

    for g, (window, dil) in enumerate(A_GROUPS):
        C = window // dil
        q_ref, k_ref, v_ref = qkv_refs[3 * g:3 * g + 3]
        n_blocks = seq // dil // C
        first_group = g == 0

        bkt = bkt_ref[g]
        bias = jnp.full((C, 2 * C), NEG, F32)
        for b in range(N_BUCKETS):
            bias = jnp.where(bkt == b, tab_ref[(g * n_heads + h) * N_BUCKETS + b], bias)

        def attend(c, r, first_block, bias=bias, q_ref=q_ref, k_ref=k_ref, v_ref=v_ref, C=C, dil=dil,
                   first_group=first_group):
            lanes = slice(r * LANES, (r + 1) * LANES)
            if first_block:
                q, kk, vv, bias_blk = q_ref[0:C, lanes], k_ref[0:C, lanes], v_ref[0:C, lanes], bias[:, C:]
            else:
                r0 = c * C if isinstance(c, int) else pl.multiple_of(c * C, C)
                rp = (c - 1) * C if isinstance(c, int) else pl.multiple_of((c - 1) * C, C)
                q, kk, vv, bias_blk = (q_ref[pl.ds(r0, C), lanes], k_ref[pl.ds(rp, 2 * C), lanes],
                                       v_ref[pl.ds(rp, 2 * C), lanes], bias)
            s = _dot_nt(q, kk) + bias_blk
            m = jnp.max(s, axis=-1, keepdims=True)
            p = jnp.exp2(s - m)
            o1 = _dot(p.astype(BF16), jnp.concatenate([vv, jnp.ones_like(vv)], axis=1))

            if dil == 1:
                pos = pl.ds(0 if first_block else r0, C)
            else:
                pos = pl.ds((0 if first_block else c * C) * dil + r, C, stride=dil)
            if first_group:
                acc_scr[0, pos, :] = o1[:, 0:LANES]
                acc_scr[1, pos, :] = o1[:, LANES:2 * LANES]
                m_scr[pos, :] = jnp.broadcast_to(m, (C, LANES))
            else:
                m_old = m_scr[pos, :]
                m_new = jnp.maximum(m_old, m)
                a = jnp.exp2(m_old - m_new)
                b = jnp.exp2(m - m_new)
                acc_scr[0, pos, :] = a * acc_scr[0, pos, :] + b * o1[:, 0:LANES]
                acc_scr[1, pos, :] = a * acc_scr[1, pos, :] + b * o1[:, LANES:2 * LANES]
                m_scr[pos, :] = m_new

        if dil == 1:
            per_iter = min(ATTN_BLOCKS_PER_ITER, n_blocks)
            assert n_blocks % per_iter == 0
            for u in range(per_iter):
                attend(u, 0, u == 0)

            def body(bi, carry, attend=attend, per_iter=per_iter):
                for u in range(per_iter):
                    attend(bi * per_iter + u, 0, False)
                return carry

            lax.fori_loop(1, n_blocks // per_iter, body, 0)
        else:
            for r in range(dil):
                for c in range(n_blocks):
                    attend(c, r, c == 0)

    rows = ATTN_BLOCKS_PER_ITER * LANES
    for i in range(seq // rows):
        sl = slice(i * rows, (i + 1) * rows)
        o_ref[sl, :] = (acc_scr[0, sl, :] / acc_scr[1, sl, :]).astype(o_ref.dtype)


def _dilated_attention(qkvs, rel_bias, batch, seq, n_heads):
    tab = (jnp.transpose(rel_bias, (1, 2, 0)) * math.log2(math.e)).reshape(-1).astype(F32)
    bkt = jnp.asarray(_band_buckets())
    C = A_GROUPS[0][0] // A_GROUPS[0][1]

    in_specs = [pl.BlockSpec(memory_space=pltpu.SMEM),
                pl.BlockSpec(bkt.shape, lambda b, h: (0, 0, 0))]
    args = [tab, bkt]
    for g, (window, dil) in enumerate(A_GROUPS):
        assert window // dil == C and seq % (dil * C) == 0
        for s in range(3):
            in_specs.append(pl.BlockSpec((None, None, seq // dil, dil * LANES),
                                         lambda b, h, s=s: (s * n_heads + h, b, 0, 0)))
            args.append(qkvs[g])
    vmem = len(A_GROUPS) * 3 * 2 * seq * LANES * 2 + 2 * seq * LANES * 2 + seq * 3 * LANES * 4
    return pl.pallas_call(
        functools.partial(_dilated_body, n_heads=n_heads, seq=seq),
        grid=(batch, n_heads),
        in_specs=in_specs,
        out_specs=pl.BlockSpec((seq, LANES), lambda b, h: (b, h)),
        out_shape=jax.ShapeDtypeStruct((batch * seq, n_heads * LANES), BF16),
        scratch_shapes=[pltpu.VMEM((2, seq, LANES), F32), pltpu.VMEM((seq, LANES), F32)],
        compiler_params=_params(("parallel", "parallel"), vmem),
        name="dilated_attention",
    )(*args)


def _mm_res_body(x_ref, w_ref, res_ref, out_ref):
    out_ref[...] = res_ref[...] + _dot(x_ref[...], w_ref[...])


def _mm_res(x, w_all, layer, res, bm_pref, bn_pref):
    M, K = x.shape
    N = w_all.shape[2]
    bm = _pick(M, bm_pref, BF16_SUBLANES)
    bn = _pick(N, bn_pref, LANES)
    vmem = 2 * bm * K * 2 + 2 * K * bn * 2 + 5 * bm * bn * 4
    if K > N:
        grid = (N // bn, M // bm)
        mn = lambda a, b: (b, a)
    else:
        grid = (M // bm, N // bn)
        mn = lambda a, b: (a, b)
    return pl.pallas_call(
        _mm_res_body,
        grid=grid,
        in_specs=[pl.BlockSpec((bm, K), lambda a, b: (mn(a, b)[0], 0)),
                  pl.BlockSpec((None, K, bn), lambda a, b: (layer, 0, mn(a, b)[1])),
                  pl.BlockSpec((bm, bn), lambda a, b: mn(a, b))],
        out_specs=pl.BlockSpec((bm, bn), lambda a, b: mn(a, b)),
        out_shape=jax.ShapeDtypeStruct((M, N), F32),
        compiler_params=_params(("parallel", "parallel"), vmem),
        name="mm_res",
    )(x, w_all, res)


def _mm_res_norm_body(x_ref, w_ref, res_ref, g_ref, h_ref, hn_ref):
    h = res_ref[...] + _dot(x_ref[...], w_ref[...])
    h_ref[...] = h
    hn_ref[...] = (h * _rms_scale(h, h.shape[-1]) * g_ref[...]).astype(hn_ref.dtype)


def _mm_res_norm(x, w_all, layer, res, gain):
    M, K = x.shape
    N = w_all.shape[2]
    bm = _pick(M, 256, BF16_SUBLANES)
    vmem = 2 * bm * K * 2 + K * N * 2 + 5 * bm * N * 4 + 2 * bm * N * 2
    row_spec = pl.BlockSpec((bm, N), lambda m: (m, 0))
    return pl.pallas_call(
        _mm_res_norm_body,
        grid=(M // bm,),
        in_specs=[pl.BlockSpec((bm, K), lambda m: (m, 0)),
                  pl.BlockSpec((None, K, N), lambda m: (layer, 0, 0), pipeline_mode=pl.Buffered(1)),
                  row_spec,
                  pl.BlockSpec((1, N), lambda m: (0, 0))],
        out_specs=[row_spec, row_spec],
        out_shape=[jax.ShapeDtypeStruct((M, N), F32), jax.ShapeDtypeStruct((M, N), BF16)],
        compiler_params=_params(("parallel",), vmem),
        name="mm_res_norm",
    )(x, w_all, res, gain)


def _ffn_up_body(x_hbm, wg_ref, wu_ref, cp_ref, o_ref,
                 lhs_scr, w_scr, accg_scr, accu_scr, lhs_sem, *, blocks_per_seq, nb, n_tiles):
    i = pl.program_id(0)
    m = i // nb
    n = i % nb
    bm = o_ref.shape[0]
    first_of_seq = (m % blocks_per_seq) == 0
    new_rows = (n == 0) & (i < n_tiles)

    @pl.when(i == 0)
    def _():
        accg_scr[...] = jnp.zeros(accg_scr.shape, accg_scr.dtype)
        accu_scr[...] = jnp.zeros(accu_scr.shape, accu_scr.dtype)

    def lhs_copy(src_start, n_rows, dst_start):
        return pltpu.make_async_copy(x_hbm.at[pl.ds(src_start, n_rows), :],
                                     lhs_scr.at[pl.ds(dst_start, n_rows), :], lhs_sem)

    @pl.when(new_rows & first_of_seq)
    def _():
        copy = lhs_copy(pl.multiple_of(m * bm, bm), bm, HALO)
        copy.start()
        lhs_scr[:HALO, :] = jnp.zeros((HALO, lhs_scr.shape[1]), lhs_scr.dtype)
        copy.wait()

    @pl.when(new_rows & jnp.logical_not(first_of_seq))
    def _():
        copy = lhs_copy(pl.multiple_of(m * bm - HALO, HALO), bm + HALO, 0)
        copy.start()
        copy.wait()

    bn = o_ref.shape[1]
    w_scr[:, 0:bn] = wg_ref[...].astype(w_scr.dtype)
    w_scr[:, bn:2 * bn] = wu_ref[...].astype(w_scr.dtype)

    def conv(acc, cp):
        a1 = pltpu.roll(acc, 1, 0)
        a2 = pltpu.roll(acc, 2, 0)
        return acc[HALO:] * cp[2:3, :] + a1[HALO:] * cp[1:2, :] + a2[HALO:] * cp[0:1, :] + cp[3:4, :]

    n_prev = jnp.maximum(i - 1, 0) % nb
    cp_gate = cp_ref[n_prev]
    cp_up = cp_ref[n_prev + nb]

    rc = bm // FFN_ROW_CHUNKS
    for c in range(FFN_ROW_CHUNKS):
        rows = pl.ds(c * rc, rc + HALO)
        gate = conv(accg_scr[rows, :], cp_gate)
        up = conv(accu_scr[rows, :], cp_up)
        o_ref[pl.ds(c * rc, rc), :] = (gate / (1.0 + jnp.exp(-gate)) * up).astype(o_ref.dtype)

        n_rows = rc + (HALO if c == FFN_ROW_CHUNKS - 1 else 0)
        lhs = lhs_scr[pl.ds(c * rc, n_rows), :]
        res = _dot(lhs, w_scr[...])
        accg_scr[pl.ds(c * rc, n_rows), :] = res[:, 0:bn]
        accu_scr[pl.ds(c * rc, n_rows), :] = res[:, bn:2 * bn]


def _ffn_up(hn, w_up, conv_w, conv_b, layer, seq):
    T, D = hn.shape
    n_layers = w_up.shape[0]
    F = w_up.shape[2] // 2
    bm = _pick(seq, 2048, BF16_SUBLANES * FFN_ROW_CHUNKS)
    bn = _pick(F, 256, LANES)
    nb = F // bn
    n_tiles = (T // bm) * nb
    conv_params = jnp.concatenate([conv_w, conv_b[:, None, :]], axis=1)
    conv_params = conv_params.reshape(n_layers, CONV_WIDTH + 1, 2 * nb, bn).transpose(0, 2, 1, 3)
    vmem = ((bm + HALO) * D * 2 + 4 * D * bn * 4 + 2 * D * bn * 2 + 2 * bm * bn * 2
            + 4 * (bm + HALO) * bn * 4 + 4 * nb * 8 * bn * 4)
    body = functools.partial(_ffn_up_body, blocks_per_seq=seq // bm, nb=nb, n_tiles=n_tiles)

    def cur(i):
        return jnp.minimum(i, n_tiles - 1)

    def prev(i):
        return jnp.maximum(i - 1, 0)

    return pl.pallas_call(
        body,
        grid=(n_tiles + 1,),
        in_specs=[pl.BlockSpec(memory_space=pl.ANY),
                  pl.BlockSpec((None, D, bn), lambda i: (layer, 0, cur(i) % nb)),
                  pl.BlockSpec((None, D, bn), lambda i: (layer, 0, cur(i) % nb + nb)),
                  pl.BlockSpec((None, 2 * nb, CONV_WIDTH + 1, bn), lambda i: (layer, 0, 0, 0))],
        out_specs=pl.BlockSpec((bm, bn), lambda i: (prev(i) // nb, prev(i) % nb)),
        out_shape=jax.ShapeDtypeStruct((T, F), BF16),
        scratch_shapes=[pltpu.VMEM((bm + HALO, D), BF16),
                        pltpu.VMEM((D, 2 * bn), BF16),
                        pltpu.VMEM((bm + HALO, bn), F32),
                        pltpu.VMEM((bm + HALO, bn), F32),
                        pltpu.SemaphoreType.DMA(())],
        compiler_params=_params(("arbitrary",), vmem),
        name="ffn_up",
    )(hn, w_up, w_up, conv_params)


def _rope_tables(seq):
    pos = jnp.arange(seq, dtype=F32)
    inv = ROPE_THETA ** (-jnp.arange(0, ROPE_DIM, 2, dtype=F32) / ROPE_DIM)
    ang = pos[:, None] * inv[None, :]
    cos, sin = jnp.cos(ang), jnp.sin(ang)
    zeros = jnp.zeros((seq, LANES - ROPE_DIM), F32)
    return (jnp.concatenate([cos, cos, zeros], axis=-1), jnp.concatenate([-sin, sin, zeros], axis=-1))


def _with_swapped_rope(w_rope):
    half = ROPE_DIM // 2
    return jnp.concatenate([w_rope, w_rope[..., half:], w_rope[..., :half]], axis=-1)


def _rope(y, c_tab, s_tab):
    return y * c_tab + pltpu.roll(y, ROPE_DIM, 1) * s_tab


def _latent_down_body(x_ref, g_ref, wkv_ref, gl_ref, gr_ref, c_ref, s_ref, wq_ref, gq_ref,
                      ckv_ref, kr_ref, ssq_ref, cq_ref, *, kv_rank):
    x = x_ref[...]
    y = x * _rms_scale(x, x.shape[-1])

    acc = _dot((y * g_ref[0:1, :]).astype(BF16), wkv_ref[...])
    c = acc[:, :kv_rank]
    ckv_ref[...] = (c * _rms_scale(c, kv_rank) * gl_ref[...]).astype(ckv_ref.dtype)
    kr = acc[:, kv_rank:kv_rank + LANES]
    ssq_ref[...] = jnp.broadcast_to(0.5 * jnp.sum(kr * kr, axis=-1, keepdims=True), kr.shape)
    kr_ref[...] = _rope(kr * gr_ref[...], c_ref[...], s_ref[...])

    acc_q = _dot((y * g_ref[1:2, :]).astype(BF16), wq_ref[...])
    cq_ref[...] = (acc_q * _rms_scale(acc_q, acc_q.shape[-1]) * gq_ref[...]).astype(cq_ref.dtype)


def _latent_down(h, gains, w_kv, latent_gain, rope_gain_pad, tables, w_dq_all, layer, q_latent_gain, seq, kv_rank):
    T, D = h.shape
    n_kv = w_kv.shape[1]
    q_rank = w_dq_all.shape[2]
    bm = _pick(seq, 512, BF16_SUBLANES)
    spb = seq // bm
    vmem = 4 * bm * D * 4 + 2 * bm * D * 2 + D * (n_kv + q_rank) * 2 + 6 * bm * (n_kv + q_rank) * 4
    tab_spec = pl.BlockSpec((bm, LANES), lambda m: (m % spb, 0))
    return pl.pallas_call(
        functools.partial(_latent_down_body, kv_rank=kv_rank),
        grid=(T // bm,),
        in_specs=[pl.BlockSpec((bm, D), lambda m: (m, 0)),
                  pl.BlockSpec((2, D), lambda m: (0, 0)),
                  pl.BlockSpec((D, n_kv), lambda m: (0, 0), pipeline_mode=pl.Buffered(1)),
                  pl.BlockSpec((1, kv_rank), lambda m: (0, 0)),
                  pl.BlockSpec((1, LANES), lambda m: (0, 0)),
                  tab_spec, tab_spec,
                  pl.BlockSpec((None, D, q_rank), lambda m: (layer, 0, 0), pipeline_mode=pl.Buffered(1)),
                  pl.BlockSpec((1, q_rank), lambda m: (0, 0))],
        out_specs=[pl.BlockSpec((bm, kv_rank), lambda m: (m, 0)),
                   pl.BlockSpec((bm, LANES), lambda m: (m, 0)),
                   pl.BlockSpec((bm, LANES), lambda m: (m, 0)),
                   pl.BlockSpec((bm, q_rank), lambda m: (m, 0))],
        out_shape=[jax.ShapeDtypeStruct((T, kv_rank), BF16),
                   jax.ShapeDtypeStruct((T, LANES), F32),
                   jax.ShapeDtypeStruct((T, LANES), F32),
                   jax.ShapeDtypeStruct((T, q_rank), BF16)],
        compiler_params=_params(("parallel",), vmem),
        name="latent_down",
    )(h, gains, w_kv, latent_gain, rope_gain_pad, *tables, w_dq_all, q_latent_gain)


def _kv_up_body(c_ref, w_ref, kr_ref, ssq_ref, gn_ref, k_ref, v_ref, *, heads_per_blk):
    acc = _dot(c_ref[...], w_ref[...])
    kr = kr_ref[...]
    ssq = ssq_ref[...]
    width = NOPE_DIM + V_DIM
    for j in range(heads_per_blk):
        kn = acc[:, j * width:j * width + NOPE_DIM]
        f = lax.rsqrt((jnp.sum(kn * kn, axis=-1, keepdims=True) + ssq) / QK_DIM + EPS)
        k_ref[j, :, 0:NOPE_DIM] = (kn * f * gn_ref[...]).astype(k_ref.dtype)
        k_ref[j, :, NOPE_DIM:2 * LANES] = (kr * f).astype(k_ref.dtype)
        v_ref[j, :, 0:V_DIM] = acc[:, j * width + NOPE_DIM:(j + 1) * width].astype(v_ref.dtype)
        v_ref[j, :, V_DIM:2 * LANES] = jnp.ones((acc.shape[0], 2 * LANES - V_DIM), v_ref.dtype)


def _kv_up(ckv, w, kr, ssq, nope_gain, n_heads):
    T, K = ckv.shape
    width = NOPE_DIM + V_DIM
    bm = _pick(T, 1024, BF16_SUBLANES)
    hpb = _pick(n_heads, 8, 1)
    bn = hpb * width
    vmem = 2 * bm * K * 2 + 2 * K * bn * 2 + 4 * bm * LANES * 4 + 2 * bm * bn * 4 + 2 * hpb * bm * 4 * LANES * 2
    return pl.pallas_call(
        functools.partial(_kv_up_body, heads_per_blk=hpb),
        grid=(T // bm, n_heads // hpb),
        in_specs=[pl.BlockSpec((bm, K), lambda m, n: (m, 0)),
                  pl.BlockSpec((K, bn), lambda m, n: (0, n)),
                  pl.BlockSpec((bm, LANES), lambda m, n: (m, 0)),
                  pl.BlockSpec((bm, LANES), lambda m, n: (m, 0)),
                  pl.BlockSpec((1, NOPE_DIM), lambda m, n: (0, 0))],
        out_specs=[pl.BlockSpec((hpb, bm, 2 * LANES), lambda m, n: (n, m, 0)),
                   pl.BlockSpec((hpb, bm, 2 * LANES), lambda m, n: (n, m, 0))],
        out_shape=[jax.ShapeDtypeStruct((n_heads, T, 2 * LANES), BF16),
                   jax.ShapeDtypeStruct((n_heads, T, 2 * LANES), BF16)],
        compiler_params=_params(("parallel", "parallel"), vmem),
        name="kv_up",
    )(ckv, w, kr, ssq, nope_gain)


def _q_down_body(x_ref, w_ref, g_ref, o_ref):
    acc = _dot(x_ref[...], w_ref[...])
    o_ref[...] = (acc * _rms_scale(acc, acc.shape[-1]) * g_ref[...]).astype(o_ref.dtype)


def _q_down(hn, w_all, layer, gain):
    T, D = hn.shape
    N = w_all.shape[2]
    bm = _pick(T, 1024, BF16_SUBLANES)
    vmem = 2 * bm * D * 2 + 2 * D * N * 2 + 3 * bm * N * 4 + 2 * bm * N * 2
    return pl.pallas_call(
        _q_down_body,
        grid=(T // bm,),
        in_specs=[pl.BlockSpec((bm, D), lambda m: (m, 0)),
                  pl.BlockSpec((None, D, N), lambda m: (layer, 0, 0)),
                  pl.BlockSpec((1, N), lambda m: (0, 0))],
        out_specs=pl.BlockSpec((bm, N), lambda m: (m, 0)),
        out_shape=jax.ShapeDtypeStruct((T, N), BF16),
        compiler_params=_params(("parallel",), vmem),
        name="q_down",
    )(hn, w_all, gain)


def _q_up_body(c_ref, w_ref, gn_ref, gr_ref, ct_ref, st_ref, q_ref, acc_scr, *, heads_per_blk):
    @pl.when(pl.program_id(0) == 0)
    def _():
        acc_scr[...] = jnp.zeros(acc_scr.shape, acc_scr.dtype)

    rc = acc_scr.shape[0] // ROW_CHUNKS
    for c in range(ROW_CHUNKS):
        rows = pl.ds(c * rc, rc)
        ct, st = ct_ref[rows, :], st_ref[rows, :]
        for j in range(heads_per_blk):
            tn = acc_scr[rows, 2 * j * LANES:(2 * j + 1) * LANES]
            tr = acc_scr[rows, (2 * j + 1) * LANES:(2 * j + 2) * LANES]
            ssq = jnp.sum(tn * tn + 0.5 * (tr * tr), axis=-1, keepdims=True)
            f = lax.rsqrt(ssq / QK_DIM + EPS)
            q_ref[j, rows, 0:LANES] = (tn * f * gn_ref[...]).astype(q_ref.dtype)
            q_ref[j, rows, LANES:2 * LANES] = (_rope(tr * gr_ref[...], ct, st) * f).astype(q_ref.dtype)
        acc_scr[rows, :] = _dot(c_ref[rows, :], w_ref[...])


def _q_up(cq, w_pad, nope_gain, rope_gain_pad, tables, seq, n_heads):
    T, K = cq.shape
    bm = _pick(seq, 1024, BF16_SUBLANES * ROW_CHUNKS)
    spb = seq // bm
    hpb = _pick(n_heads, 8, 1)
    bn = hpb * 2 * LANES
    nb = n_heads // hpb
    n_tiles = (T // bm) * nb
    vmem = 2 * bm * K * 2 + 2 * K * bn * 2 + 6 * bm * LANES * 4 + 3 * bm * bn * 4 + 2 * bm * bn * 2

    def cur(i):
        return jnp.minimum(i, n_tiles - 1)

    def prev(i):
        return jnp.maximum(i - 1, 0)

    tab_spec = pl.BlockSpec((bm, LANES), lambda i: (prev(i) // nb % spb, 0))
    return pl.pallas_call(
        functools.partial(_q_up_body, heads_per_blk=hpb),
        grid=(n_tiles + 1,),
        in_specs=[pl.BlockSpec((bm, K), lambda i: (cur(i) // nb, 0)),
                  pl.BlockSpec((K, bn), lambda i: (0, cur(i) % nb)),
                  pl.BlockSpec((1, LANES), lambda i: (0, 0)),
                  pl.BlockSpec((1, LANES), lambda i: (0, 0)),
                  tab_spec, tab_spec],
        out_specs=pl.BlockSpec((hpb, bm, 2 * LANES), lambda i: (prev(i) % nb, prev(i) // nb, 0)),
        out_shape=jax.ShapeDtypeStruct((n_heads, T, 2 * LANES), BF16),
        scratch_shapes=[pltpu.VMEM((bm, bn), F32)],
        compiler_params=_params(("arbitrary",), vmem),
        name="q_up",
    )(cq, w_pad, nope_gain, rope_gain_pad, *tables)


def _mla_body(q_ref, k_ref, v_ref, o_ref, *, blk, bk, heads_per_step):
    qi = pl.program_id(2)
    qs = [q_ref[t] for t in range(heads_per_step)]

    def chunk(j, carry, diag_offset=None):
        k0 = pl.multiple_of(j * bk, bk)

        ss = [_dot_nt(qs[t], k_ref[t, pl.ds(k0, bk), :]) for t in range(heads_per_step)]
        if diag_offset is not None:
            row = lax.broadcasted_iota(jnp.int32, ss[0].shape, 0)
            col = lax.broadcasted_iota(jnp.int32, ss[0].shape, 1) + diag_offset
            ss = [jnp.where(col <= row, s, NEG) for s in ss]
        ms = [jnp.maximum(carry[t][0], jnp.max(ss[t], axis=-1, keepdims=True)) for t in range(heads_per_step)]
        ps = [jnp.exp2(ss[t] - ms[t]).astype(BF16) for t in range(heads_per_step)]
        out = []
        for t in range(heads_per_step):
            m, acc = carry[t]
            acc = jnp.exp2(m - ms[t]) * acc + _dot(ps[t], v_ref[t, pl.ds(k0, bk), :])
            out.append((ms[t], acc))
        return tuple(out)

    per_blk = blk // bk
    carry = tuple((jnp.full((blk, 1), NEG, F32), jnp.zeros((blk, 2 * LANES), F32))
                  for _ in range(heads_per_step))
    carry = lax.fori_loop(0, qi * per_blk, chunk, carry)
    for d in range(per_blk):
        carry = chunk(qi * per_blk + d, carry, diag_offset=d * bk)
    for t in range(heads_per_step):
        acc = carry[t][1]
        o_ref[:, t * V_DIM:(t + 1) * V_DIM] = (acc[:, :V_DIM] / acc[:, V_DIM:2 * V_DIM]).astype(o_ref.dtype)


def _mla_attention(q, k, v1, batch, seq):
    n_heads, T, _ = q.shape
    blk = _pick(seq, 512, BF16_SUBLANES)
    nq = seq // blk
    hps = _pick(n_heads, 4, 1)
    vmem = hps * (2 * blk * 2 * LANES * 2 + 2 * seq * 4 * LANES * 2 + 2 * blk * V_DIM * 2 + 6 * blk * blk * 4)
    return pl.pallas_call(
        functools.partial(_mla_body, blk=blk, bk=_pick(blk, MLA_KEY_CHUNK, BF16_SUBLANES), heads_per_step=hps),
        grid=(batch, n_heads // hps, nq),
        in_specs=[pl.BlockSpec((hps, blk, 2 * LANES), lambda b, h, i: (h, b * nq + i, 0)),
                  pl.BlockSpec((hps, seq, 2 * LANES), lambda b, h, i: (h, b, 0)),
                  pl.BlockSpec((hps, seq, 2 * LANES), lambda b, h, i: (h, b, 0))],
        out_specs=pl.BlockSpec((blk, hps * V_DIM), lambda b, h, i: (b * nq + i, h)),
        out_shape=jax.ShapeDtypeStruct((T, n_heads * V_DIM), BF16),
        compiler_params=_params(("parallel", "parallel", "parallel"), vmem),
        name="mla_attention",
    )(q, k, v1)


def _out_proj_and_norm(attn, w_all, layer, h, gain):
    gain = gain.reshape(1, -1)
    if w_all.shape[1] * w_all.shape[2] * w_all.dtype.itemsize <= RESIDENT_WEIGHT_BYTES:
        return _mm_res_norm(attn, w_all, layer, h, gain)
    h = _mm_res(attn, w_all, layer, h, 1024, 512)
    return h, _rmsnorm(h, gain)[0]


def _conv_ffn(h, hn, w_up, conv_w, conv_b, w_down, layer, seq):
    act = _ffn_up(hn, w_up, conv_w, conv_b, layer, seq)
    return _mm_res(act, w_down, layer, h, 512, 512)


def kernel(x, a_attn_norm, a_w_qkv, a_q_norm, a_k_norm, a_w_o, rel_bias, kv_norm, kv_w_down,
           kv_latent_norm, kv_w_up, kv_k_norm, b_attn_norm, b_w_dq, b_q_latent_norm, b_w_uq, b_q_norm,
           b_w_o, ffn_norm, ffn_w_up, ffn_conv_w, ffn_conv_b, ffn_w_down):
    batch, seq, d_model = x.shape
    T = batch * seq
    n_g = len(A_GROUPS)
    n_a_layers = a_w_qkv.shape[0]
    n_b_layers = b_w_dq.shape[0]
    a_heads = a_w_qkv.shape[2] // (3 * n_g * LANES)
    b_heads = b_w_o.shape[1] // V_DIM
    kv_rank = kv_w_down.shape[1] - ROPE_DIM
    tables = _rope_tables(seq)

    a_w_qkv_bf, a_w_o_bf, b_w_dq_bf, b_w_o_bf = (w.astype(BF16) for w in (a_w_qkv, a_w_o, b_w_dq, b_w_o))
    ffn_w_down_bf = ffn_w_down.astype(BF16)

    h = x.reshape(T, d_model)
    k_shared = v_shared = None
    for layer in range(n_a_layers + n_b_layers):
        if layer < n_a_layers:
            (hn,) = _rmsnorm(h, a_attn_norm[layer].reshape(1, -1))
            q_scale = LANES ** -0.5 * math.log2(math.e)
            qkvs = []
            for g, (_, dil) in enumerate(A_GROUPS):
                gains = jnp.stack([a_q_norm[layer, g] * q_scale, a_k_norm[layer, g], jnp.ones((LANES,), F32)])
                qkvs.append(_qkv_proj(hn, a_w_qkv_bf, layer, g, gains, batch, seq, dil))
            attn = _dilated_attention(qkvs, rel_bias, batch, seq, a_heads)
            h, hn = _out_proj_and_norm(attn, a_w_o_bf, layer, h, ffn_norm[layer])
        else:
            i = layer - n_a_layers
            if layer == n_a_layers:
                w_kvd = jnp.concatenate([kv_w_down[:, :kv_rank], _with_swapped_rope(kv_w_down[:, kv_rank:])],
                                        axis=1).astype(BF16)
                ckv, kr, ssq, cq = _latent_down(
                    h, jnp.stack([kv_norm, b_attn_norm[i]]), w_kvd, kv_latent_norm.reshape(1, -1),
                    _with_swapped_rope(kv_k_norm[NOPE_DIM:]).reshape(1, LANES), tables,
                    b_w_dq_bf, i, b_q_latent_norm[i].reshape(1, -1), seq, kv_rank)
                k_shared, v_shared = _kv_up(ckv, kv_w_up.astype(BF16), kr, ssq,
                                            kv_k_norm[:NOPE_DIM].reshape(1, -1), b_heads)
            else:
                (hn_q,) = _rmsnorm(h, b_attn_norm[i].reshape(1, -1))
                cq = _q_down(hn_q, b_w_dq_bf, i, b_q_latent_norm[i].reshape(1, -1))
            q_rank = b_w_uq.shape[1]
            w_uq = b_w_uq[i].reshape(q_rank, b_heads, QK_DIM)
            w_uq = jnp.concatenate([w_uq[..., :NOPE_DIM], _with_swapped_rope(w_uq[..., NOPE_DIM:])], axis=-1)
            w_uq = w_uq.reshape(q_rank, b_heads * 2 * LANES).astype(BF16)
            q_gain = b_q_norm[i] * (QK_DIM ** -0.5 * math.log2(math.e))
            q = _q_up(cq, w_uq, q_gain[:NOPE_DIM].reshape(1, -1),
                      _with_swapped_rope(q_gain[NOPE_DIM:]).reshape(1, LANES), tables, seq, b_heads)
            attn = _mla_attention(q, k_shared, v_shared, batch, seq)
            h, hn = _out_proj_and_norm(attn, b_w_o_bf, i, h, ffn_norm[layer])
        h = _conv_ffn(h, hn, ffn_w_up, ffn_conv_w, ffn_conv_b, ffn_w_down_bf, layer, seq)
    return h.reshape(batch, seq, d_model)
```
